```python
import math
import jax, jax.numpy as jnp
from jax import lax
import numpy as np

D_MODEL = 1024
BATCH = 4
SEQ = 4096
DEPTH = 4

GRID_W = 64
CTX_LEN = 256

D_MIX = D_MODEL
N_MIXERS = 4
D_GROUP = D_MIX // N_MIXERS
CONV_W = 7

SSD_HEADS = 4
SSD_HEAD_DIM = D_GROUP // SSD_HEADS
SSD_GROUPS = 2
SSD_STATE = 64
SSD_CHUNK = 128
SSD_CONV_DIM = D_GROUP + 2 * SSD_GROUPS * SSD_STATE
DT_MIN = 1e-3
DT_MAX = 1e-1

POOL_WINDOWS = (2, 4, 8, 16)
POOL_GROUP = D_GROUP // len(POOL_WINDOWS)

GDN_HEADS = 4
GDN_HEAD_DIM = D_GROUP // GDN_HEADS
GDN_CHUNK = 64

GMLP_GROUPS = 4
GMLP_CHUNK = 128
GMLP_GROUP_DIM = D_GROUP // GMLP_GROUPS

D_FF = 2816
FFN_CONV_W = 3

DN_ALPHA = (2 * DEPTH) ** 0.25
DN_BETA = (8 * DEPTH) ** -0.25
LN_EPS = 1e-6
RMS_EPS = 1e-6

IN_SPLITS = (D_GROUP,
             SSD_CONV_DIM,
             2 * SSD_HEADS,
             D_GROUP,
             3 * D_GROUP,
             D_GROUP,
             2 * GDN_HEADS,
             2 * GDN_HEADS,
             2 * D_GROUP)
N_IN = sum(IN_SPLITS)

kernel_name = "hymba_style_ssd_pool_deltanet_gmlp_dit_block"


def split_last(t, sizes):
    idx = np.cumsum(sizes)[:-1].tolist()
    return jnp.split(t, idx, axis=-1)


def layer_norm(x, w=None, b=None):
    xf = x.astype(jnp.float32)
    mu = jnp.mean(xf, -1, keepdims=True)
    var = jnp.mean(jnp.square(xf - mu), -1, keepdims=True)
    y = (xf - mu) * lax.rsqrt(var + LN_EPS)
    if w is not None:
        y = y * w.astype(jnp.float32) + b.astype(jnp.float32)
    return y.astype(x.dtype)


def rms_norm(x, w):
    xf = x.astype(jnp.float32)
    y = xf * lax.rsqrt(jnp.mean(xf * xf, -1, keepdims=True) + RMS_EPS) * w.astype(jnp.float32)
    return y.astype(x.dtype)


def l2_normalize(x):
    xf = x.astype(jnp.float32)
    return xf * lax.rsqrt(jnp.sum(xf * xf, -1, keepdims=True) + 1e-6)


def adaln(x, shift, scale):
    return layer_norm(x) * (1 + scale) + shift


def dwconv(x, w, b=None):
    k, ch = w.shape
    y = lax.conv_general_dilated(x, w[:, None, :].astype(x.dtype), window_strides=(1,),
                                 padding=[(k // 2, k // 2)], dimension_numbers=('NWC', 'WIO', 'NWC'),
                                 feature_group_count=ch)
    return y if b is None else y + b


def flip(t):
    return jnp.flip(t, axis=1)


def ssd_scan(x, dt, a, bm, cm, h0, return_y):
    bsz, seq, nh, hp = x.shape
    q = SSD_CHUNK
    nc = seq // q
    x = x.reshape(bsz, nc, q, nh, hp)
    dt = dt.reshape(bsz, nc, q, nh)
    bm = bm.reshape(bsz, nc, q, nh, -1)
    cm = cm.reshape(bsz, nc, q, nh, -1)
    cum = jnp.cumsum(dt * a, axis=2)
    last = cum[:, :, -1]
    xdt = x * dt[..., None]
    w_end = jnp.exp(last[:, :, None, :] - cum)
    chunk_states = jnp.einsum('bcqh,bcqhn,bcqhp->bchpn', w_end, bm, xdt)

    def step(h, inp):
        st, dec = inp
        return h * jnp.exp(dec)[..., None, None] + st, h

    h_final, h_starts = lax.scan(step, h0, (jnp.moveaxis(chunk_states, 1, 0), jnp.moveaxis(last, 1, 0)))
    if not return_y:
        return None, h_final
    h_starts = jnp.moveaxis(h_starts, 0, 1)
    pos = jnp.arange(q)
    tri = pos[:, None] >= pos[None, :]
    seg = jnp.where(tri[None, None, :, :, None], cum[:, :, :, None, :] - cum[:, :, None, :, :], -jnp.inf)
    scores = jnp.einsum('bcihn,bcjhn->bcijh', cm, bm) * jnp.exp(seg)
    y_intra = jnp.einsum('bcijh,bcjhp->bcihp', scores, xdt)
    y_inter = jnp.einsum('bcihn,bchpn->bcihp', cm, h_starts) * jnp.exp(cum)[..., None]
    return (y_intra + y_inter).reshape(bsz, seq, nh, hp), h_final


def ssd_mixer(z, xbc, dt_raw, conv_w, conv_b, dt_bias, a_log, d_skip, norm_w, h0_f, h0_b, return_y):
    bsz, seq, _ = xbc.shape
    xbc = jax.nn.silu(dwconv(xbc, conv_w, conv_b)).astype(jnp.float32)
    xs, bm, cm = split_last(xbc, (D_GROUP, SSD_GROUPS * SSD_STATE, SSD_GROUPS * SSD_STATE))
    xs = xs.reshape(bsz, seq, SSD_HEADS, SSD_HEAD_DIM)
    rep = SSD_HEADS // SSD_GROUPS
    bm = jnp.repeat(bm.reshape(bsz, seq, SSD_GROUPS, SSD_STATE), rep, axis=2)
    cm = jnp.repeat(cm.reshape(bsz, seq, SSD_GROUPS, SSD_STATE), rep, axis=2)
    dt = jax.nn.softplus(dt_raw.astype(jnp.float32).reshape(bsz, seq, 2, SSD_HEADS)
                         + dt_bias.astype(jnp.float32))
    a = -jnp.exp(a_log.astype(jnp.float32))
    y_f, h_f = ssd_scan(xs, dt[:, :, 0], a[0], bm, cm, h0_f, return_y)
    y_b, h_b = ssd_scan(flip(xs), flip(dt[:, :, 1]), a[1], flip(bm), flip(cm), h0_b, return_y)
    if not return_y:
        return None, h_f, h_b
    y = y_f + flip(y_b) + xs * d_skip.astype(jnp.float32)[:, None]
    y = y.reshape(bsz, seq, D_GROUP) * jax.nn.silu(z.astype(jnp.float32))
    return rms_norm(y, norm_w), h_f, h_b


def multiscale_pool(x):
    seq = x.shape[-2]
    xf = x.astype(jnp.float32)
    csum = jnp.concatenate([jnp.zeros_like(xf[..., :1, :]), jnp.cumsum(xf, axis=-2)], axis=-2)
    pos = jnp.arange(seq)
    outs = []
    for g, w in enumerate(POOL_WINDOWS):
        lo = jnp.clip(pos - w // 2, 0, seq)
        hi = jnp.clip(pos + w - w // 2, 0, seq)
        cg = csum[..., g * POOL_GROUP:(g + 1) * POOL_GROUP]
        s = jnp.take(cg, hi, axis=-2) - jnp.take(cg, lo, axis=-2)
        outs.append(s / (hi - lo).astype(jnp.float32)[:, None])
    return jnp.concatenate(outs, axis=-1) - xf


def pool_mixer(p_in, pool_w, pool_scale, on_grid):
    bsz, seq, ch = p_in.shape
    if on_grid:
        rows = seq // GRID_W
        pooled = multiscale_pool(p_in.reshape(bsz, rows, GRID_W, ch)).reshape(bsz, seq, ch)
    else:
        pooled = multiscale_pool(p_in)
    pg = pooled.reshape(bsz, seq, len(POOL_WINDOWS), POOL_GROUP)
    y = jnp.einsum('blgc,gcd->blgd', pg, pool_w.astype(jnp.float32)).reshape(bsz, seq, ch)
    return y * pool_scale.astype(jnp.float32)


def gated_delta_chunked(q, k, v, g, beta, s0, return_o):
    bsz, seq, nh, dk = q.shape
    dv = v.shape[-1]
    cl = GDN_CHUNK
    nc = seq // cl

    def chunks(t):
        return t.reshape(bsz, nc, cl, nh, -1).transpose(0, 3, 1, 2, 4)

    q = chunks(q * dk ** -0.5)
    k = chunks(k)
    v = chunks(v)
    g = g.reshape(bsz, nc, cl, nh).transpose(0, 3, 1, 2)
    beta = beta.reshape(bsz, nc, cl, nh).transpose(0, 3, 1, 2)
    gc = jnp.cumsum(g, axis=-1)
    pos = jnp.arange(cl)
    lower = pos[:, None] >= pos[None, :]
    strict = pos[:, None] > pos[None, :]
    decay = jnp.exp(jnp.where(lower, gc[..., :, None] - gc[..., None, :], -jnp.inf))
    kb = k * beta[..., None]
    a_mat = jnp.where(strict, jnp.einsum('bhcid,bhcjd->bhcij', kb, k) * decay, 0.0)
    t_mat = a_mat + jnp.eye(cl, dtype=jnp.float32)
    rhs = jnp.concatenate([v * beta[..., None], kb * jnp.exp(gc)[..., None]], axis=-1)
    sol = lax.linalg.triangular_solve(t_mat, rhs, left_side=True, lower=True, unit_diagonal=True)
    u, w = sol[..., :dv], sol[..., dv:]
    k_tail = k * jnp.exp(gc[..., -1:] - gc)[..., None]
    g_last = gc[..., -1]

    def step(s, inp):
        u_c, w_c, kt_c, gl_c = inp
        v_new = u_c - jnp.einsum('bhik,bhkv->bhiv', w_c, s)
        s_next = s * jnp.exp(gl_c)[..., None, None] + jnp.einsum('bhik,bhiv->bhkv', kt_c, v_new)
        return s_next, (v_new, s)

    s_final, (v_new, s_starts) = lax.scan(
        step, s0, (jnp.moveaxis(u, 2, 0), jnp.moveaxis(w, 2, 0), jnp.moveaxis(k_tail, 2, 0), jnp.moveaxis(g_last, 2, 0)))
    if not return_o:
        return None, s_final
    v_new = jnp.moveaxis(v_new, 0, 2)
    s_starts = jnp.moveaxis(s_starts, 0, 2)
    attn = jnp.where(lower, jnp.einsum('bhcid,bhcjd->bhcij', q, k) * decay, 0.0)
    o = (jnp.einsum('bhcij,bhcjv->bhciv', attn, v_new)
         + jnp.einsum('bhcik,bhckv->bhciv', q * jnp.exp(gc)[..., None], s_starts))
    return o.transpose(0, 2, 3, 1, 4).reshape(bsz, seq, nh, dv), s_final


def gdn_mixer(qkv, gate, a_raw, b_raw, conv_w, dt_bias, a_log, norm_w, s0_f, s0_b, return_y):
    bsz, seq, _ = qkv.shape
    qkv = jax.nn.silu(dwconv(qkv, conv_w)).astype(jnp.float32)
    q, k, v = jnp.split(qkv, 3, axis=-1)
    shp = (bsz, seq, GDN_HEADS, GDN_HEAD_DIM)
    q = l2_normalize(q.reshape(shp))
    k = l2_normalize(k.reshape(shp))
    v = v.reshape(shp)
    g = -jnp.exp(a_log.astype(jnp.float32)) * jax.nn.softplus(
        a_raw.astype(jnp.float32).reshape(bsz, seq, 2, GDN_HEADS) + dt_bias.astype(jnp.float32))
    beta = jax.nn.sigmoid(b_raw.astype(jnp.float32).reshape(bsz, seq, 2, GDN_HEADS))
    o_f, s_f = gated_delta_chunked(q, k, v, g[:, :, 0], beta[:, :, 0], s0_f, return_y)
    o_b, s_b = gated_delta_chunked(flip(q), flip(k), flip(v), flip(g[:, :, 1]), flip(beta[:, :, 1]), s0_b, return_y)
    if not return_y:
        return None, s_f, s_b
    o = rms_norm(o_f + flip(o_b), norm_w) * jax.nn.silu(gate.astype(jnp.float32).reshape(shp))
    return o.reshape(bsz, seq, D_GROUP), s_f, s_b


def gmlp_mixer(uv, ln_w, ln_b, ws, bs):
    u, v = jnp.split(jax.nn.gelu(uv), 2, axis=-1)
    v = layer_norm(v, ln_w, ln_b)
    bsz, seq, _ = v.shape
    nc = seq // GMLP_CHUNK
    vc = v.reshape(bsz, nc, GMLP_CHUNK, GMLP_GROUPS, GMLP_GROUP_DIM)
    vs = jnp.einsum('gij,bcjgd->bcigd', ws, vc) + bs.T[None, None, :, :, None]
    return u * vs.reshape(bsz, seq, D_GROUP)


def token_mixers(h, lp, states0, on_grid, return_y):
    proj = h @ lp["w_in"]
    (ssd_z, ssd_xbc, ssd_dt, pool_in, gdn_qkv, gdn_gate, gdn_a, gdn_b, gmlp_uv) = split_last(proj, IN_SPLITS)
    ssd_h0f, ssd_h0b, gdn_s0f, gdn_s0b = states0
    y_ssd, ssd_hf, ssd_hb = ssd_mixer(ssd_z, ssd_xbc, ssd_dt, lp["ssd_conv_w"], lp["ssd_conv_b"], lp["ssd_dt_bias"],
                                      lp["ssd_a_log"], lp["ssd_d"], lp["ssd_norm_w"], ssd_h0f, ssd_h0b, return_y)
    y_gdn, gdn_sf, gdn_sb = gdn_mixer(gdn_qkv, gdn_gate, gdn_a, gdn_b, lp["gdn_conv_w"], lp["gdn_dt_bias"],
                                      lp["gdn_a_log"], lp["gdn_norm_w"], gdn_s0f, gdn_s0b, return_y)
    states = (ssd_hf, ssd_hb, gdn_sf, gdn_sb)
    if not return_y:
        return None, states
    y_pool = pool_mixer(pool_in, lp["pool_w"], lp["pool_scale"], on_grid)
    y_gmlp = gmlp_mixer(gmlp_uv, lp["gmlp_ln_w"], lp["gmlp_ln_b"], lp["gmlp_ws"], lp["gmlp_bs"])
    y = jnp.concatenate([y_ssd.astype(h.dtype), y_pool.astype(h.dtype), y_gdn.astype(h.dtype),
                         y_gmlp.astype(h.dtype)], axis=-1)
    return y, states


def conv_ffn(h, w_up, conv_w, conv_b, w_down):
    a, b = jnp.split(h @ w_up, 2, axis=-1)
    return (jax.nn.silu(dwconv(a, conv_w, conv_b)) * b) @ w_down


def stream_layer(x, mod, lp, states0, on_grid):
    sh1, sc1, g1, sh2, sc2, g2 = jnp.split(mod, 6, axis=-1)
    y, states = token_mixers(adaln(x, sh1, sc1), lp, states0, on_grid, True)
    x = layer_norm(DN_ALPHA * x + g1 * (y @ lp["w_out"]), lp["ln1_w"], lp["ln1_b"])
    f = conv_ffn(adaln(x, sh2, sc2), lp["ffn_up"], lp["ffn_conv_w"], lp["ffn_conv_b"], lp["ffn_down"])
    x = layer_norm(DN_ALPHA * x + g2 * f, lp["ln2_w"], lp["ln2_b"])
    return x, states


def zero_states(bsz):
    hs = jnp.zeros((bsz, SSD_HEADS, SSD_HEAD_DIM, SSD_STATE), jnp.float32)
    sg = jnp.zeros((bsz, GDN_HEADS, GDN_HEAD_DIM, GDN_HEAD_DIM), jnp.float32)
    return (hs, hs, sg, sg)


def setup_inputs(seed: int = 0) -> dict:
    key = jax.random.key(seed)
    ks = iter(jax.random.split(key, 48))
    L, D = DEPTH, D_MODEL

    def nrm(shape, s):
        return jax.random.normal(next(ks), shape, jnp.float32) * s

    def gain(shape):
        return 1.0 + nrm(shape, 0.02)

    def dt_bias_init(shape):
        u = jax.random.uniform(next(ks), shape, jnp.float32)
        dt = jnp.exp(u * (math.log(DT_MAX) - math.log(DT_MIN)) + math.log(DT_MIN))
        return dt + jnp.log(-jnp.expm1(-dt))

    def a_log_init(shape):
        return jnp.log(jax.random.uniform(next(ks), shape, jnp.float32, 1.0, 16.0))

    return {
        "x": nrm((BATCH, SEQ, D), 1.0),
        "c": nrm((BATCH, D), 1.0),
        "ctx": nrm((BATCH, CTX_LEN, D), 1.0),
        "c_ctx": nrm((D,), 1.0),
        "w_mod": nrm((L, D, 6 * D), 0.5 * D ** -0.5),
        "b_mod": nrm((L, 6 * D), 0.02),
        "w_in": nrm((L, D, N_IN), D ** -0.5),
        "ssd_conv_w": nrm((L, CONV_W, SSD_CONV_DIM), CONV_W ** -0.5),
        "ssd_conv_b": nrm((L, SSD_CONV_DIM), 0.02),
        "ssd_dt_bias": dt_bias_init((L, 2, SSD_HEADS)),
        "ssd_a_log": a_log_init((L, 2, SSD_HEADS)),
        "ssd_d": gain((L, SSD_HEADS)),
        "ssd_norm_w": gain((L, D_GROUP)),
        "pool_w": nrm((L, len(POOL_WINDOWS), POOL_GROUP, POOL_GROUP), POOL_GROUP ** -0.5),
        "pool_scale": gain((L, D_GROUP)),
        "gdn_conv_w": nrm((L, CONV_W, 3 * D_GROUP), CONV_W ** -0.5),
        "gdn_dt_bias": dt_bias_init((L, 2, GDN_HEADS)),
        "gdn_a_log": a_log_init((L, 2, GDN_HEADS)),
        "gdn_norm_w": gain((L, GDN_HEAD_DIM)),
        "gmlp_ln_w": gain((L, D_GROUP)),
        "gmlp_ln_b": nrm((L, D_GROUP), 0.02),
        "gmlp_ws": nrm((L, GMLP_GROUPS, GMLP_CHUNK, GMLP_CHUNK), GMLP_CHUNK ** -0.5),
        "gmlp_bs": gain((L, GMLP_GROUPS, GMLP_CHUNK)),
        "w_out": nrm((L, D_MIX, D), DN_BETA * D_MIX ** -0.5),
        "ln1_w": gain((L, D)),
        "ln1_b": nrm((L, D), 0.02),
        "ffn_up": nrm((L, D, 2 * D_FF), D ** -0.5),
        "ffn_conv_w": nrm((L, FFN_CONV_W, D_FF), FFN_CONV_W ** -0.5),
        "ffn_conv_b": nrm((L, D_FF), 0.02),
        "ffn_down": nrm((L, D_FF, D), DN_BETA * D_FF ** -0.5),
        "ln2_w": gain((L, D)),
        "ln2_b": nrm((L, D), 0.02),
    }


def reference(x, c, ctx, c_ctx, w_mod, b_mod, w_in, ssd_conv_w, ssd_conv_b, ssd_dt_bias, ssd_a_log, ssd_d,
              ssd_norm_w, pool_w, pool_scale, gdn_conv_w, gdn_dt_bias, gdn_a_log, gdn_norm_w, gmlp_ln_w, gmlp_ln_b,
              gmlp_ws, gmlp_bs, w_out, ln1_w, ln1_b, ffn_up, ffn_conv_w, ffn_conv_b, ffn_down, ln2_w, ln2_b):
    lat = x
    cx = ctx
    silu_c = jax.nn.silu(c)[:, None, :]
    silu_cc = jax.nn.silu(c_ctx)[None, None, :]
    for l in range(DEPTH):
        lp = {
            "w_in": w_in[l], "ssd_conv_w": ssd_conv_w[l], "ssd_conv_b": ssd_conv_b[l],
            "ssd_dt_bias": ssd_dt_bias[l], "ssd_a_log": ssd_a_log[l], "ssd_d": ssd_d[l],
            "ssd_norm_w": ssd_norm_w[l], "pool_w": pool_w[l], "pool_scale": pool_scale[l],
            "gdn_conv_w": gdn_conv_w[l], "gdn_dt_bias": gdn_dt_bias[l], "gdn_a_log": gdn_a_log[l],
            "gdn_norm_w": gdn_norm_w[l], "gmlp_ln_w": gmlp_ln_w[l], "gmlp_ln_b": gmlp_ln_b[l],
            "gmlp_ws": gmlp_ws[l], "gmlp_bs": gmlp_bs[l], "w_out": w_out[l], "ln1_w": ln1_w[l],
            "ln1_b": ln1_b[l], "ffn_up": ffn_up[l], "ffn_conv_w": ffn_conv_w[l], "ffn_conv_b": ffn_conv_b[l],
            "ffn_down": ffn_down[l], "ln2_w": ln2_w[l], "ln2_b": ln2_b[l],
        }
        mod_lat = silu_c @ w_mod[l] + b_mod[l]
        mod_ctx = silu_cc @ w_mod[l] + b_mod[l]
        init = zero_states(cx.shape[0])
        if l < DEPTH - 1:
            cx_next, ctx_states = stream_layer(cx, mod_ctx, lp, init, False)
        else:
            sh1, sc1 = jnp.split(mod_ctx, 6, axis=-1)[:2]
            _, ctx_states = token_mixers(adaln(cx, sh1, sc1), lp, init, False, False)
            cx_next = cx
        lat, _ = stream_layer(lat, mod_lat, lp, ctx_states, True)
        cx = cx_next
    return lat
```

```python
import functools

import jax
import jax.numpy as jnp
from jax import lax
from jax.experimental import pallas as pl
from jax.experimental.pallas import tpu as pltpu

F32 = jnp.float32
BF16 = jnp.bfloat16

D_MODEL = 1024
DEPTH = 4
D_GROUP = 256
N_HEADS = 4
HEAD_DIM = 64
SSD_CHUNK = 128
GDN_CHUNK = 64
GMLP_CHUNK = 128
GRID_W = 64
POOL_WINDOWS = (2, 4, 8, 16)
D_FF = 2816
CONV_W = 7
FFN_CONV_W = 3
CONV_C = 1280
N_PROJ = 2688
SMALL_W = 128
HALO = 8
LN_EPS = 1e-6
RMS_EPS = 1e-6
DN_ALPHA = (2 * DEPTH) ** 0.25
NEG_BIG = -1e30
VMEM_LIMIT = 56 * 1024 * 1024


def _iota(shape, dim):
    return lax.broadcasted_iota(jnp.int32, shape, dim)


def _bf(x):
    return x.astype(BF16)


def _dot(a, b):
    return jnp.dot(a, b, preferred_element_type=F32)


def _dot_nt(a, b):
    return lax.dot_general(a, b, (((1,), (1,)), ((), ())), preferred_element_type=F32)


def _dot_tn(a, b):
    return lax.dot_general(a, b, (((0,), (0,)), ((), ())), preferred_element_type=F32)


def _split3(x):
    x1 = _bf(x)
    r1 = x - x1.astype(F32)
    x2 = _bf(r1)
    x3 = _bf(r1 - x2.astype(F32))
    return x1, x2, x3


def _mm_xe(x, e):
    x1, x2, x3 = _split3(x)
    return _dot(x1, e) + _dot(x2, e) + _dot(x3, e)


def _mm_ex(e, x):
    x1, x2, x3 = _split3(x)
    return _dot(e, x1) + _dot(e, x2) + _dot(e, x3)


def _mm3(a, b):
    ah = _bf(a)
    al = _bf(a - ah.astype(F32))
    bh = _bf(b)
    bl = _bf(b - bh.astype(F32))
    return _dot(ah, bh) + _dot(ah, bl) + _dot(al, bh)


def _sigmoid(x):
    return 1.0 / (1.0 + jnp.exp(-x))


def _silu(x):
    return x * _sigmoid(x)


def _softplus(x):
    return jnp.maximum(x, 0.0) + jnp.log1p(jnp.exp(-jnp.abs(x)))


def _gelu_tanh(x):
    return x * (0.5 * (1.0 + jnp.tanh(0.7978845608028654 * (x + 0.044715 * (x * x * x)))))


def _ln(x):
    mu = jnp.mean(x, axis=-1, keepdims=True)
    xc = x - mu
    var = jnp.mean(xc * xc, axis=-1, keepdims=True)
    return xc * lax.rsqrt(var + LN_EPS)


def _head_ones():
    return jnp.where((_iota((D_GROUP, D_GROUP), 0) >> 6) == (_iota((D_GROUP, D_GROUP), 1) >> 6), 1.0, 0.0).astype(BF16)


def _cparams():
    return pltpu.CompilerParams(vmem_limit_bytes=VMEM_LIMIT)


def _resident(shape, index_map):
    return pl.BlockSpec(shape, index_map, pipeline_mode=pl.Buffered(1))


def _mod_kernel(c_ref, w_ref, b_ref, o_ref):
    o_ref[...] = _mm3(_silu(c_ref[...]), w_ref[...]) + b_ref[...]


def _modulation(cs, w_mod, b_mod):
    tn = 768
    n6 = 6 * D_MODEL
    return pl.pallas_call(
        _mod_kernel,
        out_shape=jax.ShapeDtypeStruct((DEPTH, 8, n6), F32),
        grid=(DEPTH, n6 // tn),
        in_specs=[pl.BlockSpec((8, D_MODEL), lambda l, j: (0, 0)),
                  pl.BlockSpec((None, D_MODEL, tn), lambda l, j: (l, 0, j)),
                  pl.BlockSpec((None, 1, tn), lambda l, j: (l, 0, j))],
        out_specs=pl.BlockSpec((None, 8, tn), lambda l, j: (l, 0, j)),
        compiler_params=_cparams(),
        name="modulation",
    )(cs, w_mod, b_mod.reshape(DEPTH, 1, n6))


def _in_kernel(l_ref, x_ref, mod_ref, w_ref, z_ref, cv_ref, pool_ref, gate_ref, uv_ref, sm_ref):
    del l_ref
    mod = mod_ref[...]
    h = _ln(x_ref[...]) * (1.0 + mod[:, D_MODEL:2 * D_MODEL]) + mod[:, 0:D_MODEL]
    p = _dot(_bf(h), w_ref[...])
    z_ref[...] = p[:, 0:256]
    cv_ref[...] = p[:, 256:1536]
    pool_ref[...] = p[:, 1536:1792]
    gate_ref[...] = p[:, 1792:2048]
    uv_ref[...] = p[:, 2048:2560]
    sm_ref[...] = p[:, 2560:2688]


def _mod_spec(is_ctx):
    if is_ctx:
        return pl.BlockSpec((None, None, 1, 6 * D_MODEL), lambda b, i, l: (l[0], 0, 0, 0))
    return pl.BlockSpec((None, None, 1, 6 * D_MODEL), lambda b, i, l: (l[0], b + 1, 0, 0))


def _row_spec(t, width):
    return pl.BlockSpec((None, t, width), lambda b, i, l: (b, i, 0))


def _layer_spec(shape):
    nd = len(shape)
    return _resident((None,) + tuple(shape), lambda b, i, l: (l[0],) + (0,) * nd)


def _in_proj(lidx, x, mods, w_in_p, is_ctx):
    bsz, seq, _ = x.shape
    t = min(seq, 512)
    widths = (256, CONV_C, 256, 256, 512, SMALL_W)
    grid_spec = pltpu.PrefetchScalarGridSpec(
        num_scalar_prefetch=1,
        grid=(bsz, seq // t),
        in_specs=[_row_spec(t, D_MODEL), _mod_spec(is_ctx), _layer_spec((D_MODEL, N_PROJ))],
        out_specs=[_row_spec(t, w) for w in widths],
    )
    return pl.pallas_call(
        _in_kernel,
        out_shape=[jax.ShapeDtypeStruct((bsz, seq, w), F32) for w in widths],
        grid_spec=grid_spec,
        compiler_params=_cparams(),
        name="in_proj",
    )(lidx, x, mods, w_in_p)


def _conv_taps(prev, cur, nxt, w, width):
    t = cur.shape[0]
    win = jnp.concatenate([prev, cur, nxt], axis=0)
    n = t + 2 * HALO
    half = width // 2
    acc = None
    for k in range(width):
        shift = (half - k) % n
        r = pltpu.roll(win, shift, 0) if shift else win
        term = r[HALO:HALO + t] * w[k:k + 1, :]
        acc = term if acc is None else acc + term
    return acc


def _conv_kernel(l_ref, x_ref, p_ref, n_ref, w_ref, b_ref, xbc_ref, q_ref, k_ref, v_ref):
    del l_ref
    i = pl.program_id(1)
    has_prev = i > 0
    has_next = i < pl.num_programs(1) - 1
    ones_bd = _head_ones()
    outs = (None, None, q_ref, k_ref, v_ref)
    for s in range(CONV_C // D_GROUP):
        sl = slice(s * D_GROUP, (s + 1) * D_GROUP)
        prev = jnp.where(has_prev, p_ref[:, sl], 0.0)
        nxt = jnp.where(has_next, n_ref[:, sl], 0.0)
        y = _silu(_conv_taps(prev, x_ref[:, sl], nxt, w_ref[:, sl], CONV_W) + b_ref[:, sl])
        if s < 2:
            xbc_ref[:, sl] = y
        elif s < 4:
            ss = _mm_xe(y * y, ones_bd)
            outs[s][...] = y * lax.rsqrt(ss + 1e-6)
        else:
            outs[s][...] = y


def _halo_specs(t, seq, width):
    per = t // HALO
    last = seq // HALO - 1
    prev = pl.BlockSpec((None, HALO, width), lambda b, i, l: (b, jnp.maximum(i * per - 1, 0), 0))
    nxt = pl.BlockSpec((None, HALO, width), lambda b, i, l: (b, jnp.minimum((i + 1) * per, last), 0))
    return prev, nxt


def _conv_mix(lidx, cv, conv_w, conv_b):
    bsz, seq, _ = cv.shape
    t = min(seq, 256)
    prev, nxt = _halo_specs(t, seq, CONV_C)
    grid_spec = pltpu.PrefetchScalarGridSpec(
        num_scalar_prefetch=1,
        grid=(bsz, seq // t),
        in_specs=[_row_spec(t, CONV_C), prev, nxt, _layer_spec((CONV_W, CONV_C)), _layer_spec((1, CONV_C))],
        out_specs=[_row_spec(t, 512), _row_spec(t, 256), _row_spec(t, 256), _row_spec(t, 256)],
    )
    return pl.pallas_call(
        _conv_kernel,
        out_shape=[jax.ShapeDtypeStruct((bsz, seq, w), F32) for w in (512, 256, 256, 256)],
        grid_spec=grid_spec,
        compiler_params=_cparams(),
        name="conv_mix",
    )(lidx, cv, cv, cv, conv_w, conv_b)


def _pool_gmlp_kernel(l_ref, p_ref, uv_ref, pw_ref, ps_ref, lw_ref, lb_ref, ws_ref, bs_ref, yp_ref, ym_ref, *,
                      row_w):
    del l_ref
    t = p_ref.shape[0]
    x = p_ref[...]
    lane_grp = _iota((t, D_GROUP), 1) >> 6
    ii = _iota((t, t), 0)
    jj = _iota((t, t), 1)
    base = ii - (ii & (row_w - 1))
    pos = ii & (row_w - 1)
    pos_c = _iota((t, D_GROUP), 0) & (row_w - 1)
    pooled = jnp.zeros((t, D_GROUP), F32)
    for g, w in enumerate(POOL_WINDOWS):
        lo = jnp.maximum(pos - w // 2, 0)
        hi = jnp.minimum(pos + (w - w // 2), row_w)
        band = jnp.where((jj >= base + lo) & (jj < base + hi), 1.0, 0.0).astype(BF16)
        cnt = (jnp.minimum(pos_c + (w - w // 2), row_w) - jnp.maximum(pos_c - w // 2, 0)).astype(F32)
        xg = jnp.where(lane_grp == g, x, 0.0)
        pooled = pooled + _mm_ex(band, xg) / cnt
    yp_ref[...] = _dot(_bf(pooled - x), pw_ref[...]) * ps_ref[...]

    gl = _gelu_tanh(uv_ref[...])
    u = gl[:, 0:D_GROUP]
    v = _ln(gl[:, D_GROUP:2 * D_GROUP]) * lw_ref[...] + lb_ref[...]
    grp = _iota((GMLP_CHUNK, D_GROUP), 1) >> 6
    for c in range(t // GMLP_CHUNK):
        rows = slice(c * GMLP_CHUNK, (c + 1) * GMLP_CHUNK)
        vc = v[rows]
        acc = bs_ref[...]
        for g in range(4):
            acc = acc + _dot(ws_ref[g], _bf(jnp.where(grp == g, vc, 0.0)))
        ym_ref[rows, :] = u[rows] * acc


def _pool_gmlp(lidx, pool_in, uv, pool_wbd, pool_scale, ln_w, ln_b, ws, bs_exp, on_grid):
    bsz, seq, _ = pool_in.shape
    t = min(seq, 256)
    row_w = GRID_W if on_grid else seq
    assert on_grid or t == seq
    grid_spec = pltpu.PrefetchScalarGridSpec(
        num_scalar_prefetch=1,
        grid=(bsz, seq // t),
        in_specs=[_row_spec(t, 256), _row_spec(t, 512), _layer_spec((256, 256)), _layer_spec((1, 256)),
                  _layer_spec((1, 256)), _layer_spec((1, 256)), _layer_spec((4, GMLP_CHUNK, GMLP_CHUNK)),
                  _layer_spec((GMLP_CHUNK, 256))],
        out_specs=[_row_spec(t, 256), _row_spec(t, 256)],
    )
    return pl.pallas_call(
        functools.partial(_pool_gmlp_kernel, row_w=row_w),
        out_shape=[jax.ShapeDtypeStruct((bsz, seq, 256), F32)] * 2,
        grid_spec=grid_spec,
        compiler_params=_cparams(),
        name="pool_gmlp",
    )(lidx, pool_in, uv, pool_wbd, pool_scale, ln_w, ln_b, ws, bs_exp)


def _dir_block(nb):
    return lambda d, i: jnp.where(d == 0, i, nb - 1 - i)


def _scan_specs(tb, nb):
    blk = _dir_block(nb)
    seq_spec = lambda width: pl.BlockSpec((None, tb, width), lambda b, d, i, l: (b, blk(d, i), 0))
    par_spec = lambda shape: _resident((None,) + tuple(shape), lambda b, d, i, l: (l[0],) + (0,) * len(shape))
    dir_par_spec = lambda shape: pl.BlockSpec((None, None) + tuple(shape),
                                              lambda b, d, i, l: (l[0], d) + (0,) * len(shape))
    state_spec = lambda rows: pl.BlockSpec((None, None, rows, 256), lambda b, d, i, l: (b, d, 0, 0))
    out_spec = pl.BlockSpec((None, None, tb, 256), lambda b, d, i, l: (d, b, blk(d, i), 0))
    return seq_spec, par_spec, dir_par_spec, state_spec, out_spec


def _ssd_kernel(l_ref, x_ref, sm_ref, dtb_ref, alog_ref, asm_ref, h0_ref, y_ref, hf_ref, s_ref, *, nc):
    del l_ref
    q = SSD_CHUNK
    d = pl.program_id(1)
    i = pl.program_id(2)

    @pl.when(i == 0)
    def _():
        s_ref[...] = h0_ref[...]

    fwd = d == 0
    sgn = jnp.where(fwd, 1, -1)
    rr = _iota((q, q), 0)
    cc = _iota((q, q), 1)
    tri = (rr - cc) * sgn >= 0
    tri_bf = jnp.where(tri, 1.0, 0.0).astype(BF16)
    expand = jnp.where(_iota((SMALL_W, D_GROUP), 0) == 4 * d + (_iota((SMALL_W, D_GROUP), 1) >> 6),
                       1.0, 0.0).astype(BF16)
    a_row = -jnp.exp(alog_ref[...])
    a_small = -jnp.exp(asm_ref[...])
    lane_c = _iota((q, q), 1) >> 6
    lane_h = _iota((q, D_GROUP), 1) >> 6
    state_mask = (_iota((q, D_GROUP), 0) >> 6) == (lane_h >> 1)

    def chunk(j, carry):
        c = jnp.where(fwd, j, nc - 1 - j)
        st = pl.multiple_of(c * q, q)
        dt_all = _softplus(sm_ref[pl.ds(st, q), :] + dtb_ref[...])
        dt_exp = _mm_xe(dt_all, expand)
        cum = _mm_ex(tri_bf, dt_exp * a_row)
        cum_small = _mm_ex(tri_bf, dt_all * a_small)
        cum_t = cum_small.T
        xc = x_ref[pl.ds(st, q), :]
        xs = xc[:, 0:256]
        bm = xc[:, 256:384]
        cm = xc[:, 384:512]
        bm_bf = _bf(bm)
        xdt = xs * dt_exp
        last = jnp.where(fwd, cum[q - 1:q, :], cum[0:1, :])
        s = s_ref[...]
        y = _dot(_bf(cm), _bf(s)) * jnp.exp(cum)
        gram = [_dot_nt(_bf(jnp.where(lane_c == g, cm, 0.0)), bm_bf) for g in range(2)]
        for h in range(N_HEADS):
            row = jnp.where(fwd, cum_t[h:h + 1, :], cum_t[4 + h:5 + h, :])
            sel = jnp.where(rr == 4 * d + h, 1.0, 0.0).astype(BF16)
            col = _mm_xe(cum_small, sel)
            decay = jnp.exp(jnp.where(tri, col - row, NEG_BIG))
            y = y + _dot(_bf(gram[h // 2] * decay), _bf(jnp.where(lane_h == h, xdt, 0.0)))
        y_ref[pl.ds(st, q), :] = y
        upd = _dot_tn(bm_bf, _bf(xdt * jnp.exp(last - cum)))
        s_ref[...] = s * jnp.exp(last) + jnp.where(state_mask, upd, 0.0)
        return carry

    lax.fori_loop(0, nc, chunk, 0)

    @pl.when(i == pl.num_programs(2) - 1)
    def _():
        hf_ref[...] = s_ref[...]


def _ssd_scan(lidx, xbc, small, dtb_row, alog_exp, alog_small, h0):
    bsz, seq, _ = xbc.shape
    tb = min(seq, 512)
    nb = seq // tb
    seq_spec, par_spec, dir_par_spec, state_spec, out_spec = _scan_specs(tb, nb)
    grid_spec = pltpu.PrefetchScalarGridSpec(
        num_scalar_prefetch=1,
        grid=(bsz, 2, nb),
        in_specs=[seq_spec(512), seq_spec(SMALL_W), par_spec((1, SMALL_W)), dir_par_spec((1, 256)),
                  par_spec((1, SMALL_W)), state_spec(128)],
        out_specs=[out_spec, state_spec(128)],
        scratch_shapes=[pltpu.VMEM((128, 256), F32)],
    )
    return pl.pallas_call(
        functools.partial(_ssd_kernel, nc=tb // SSD_CHUNK),
        out_shape=[jax.ShapeDtypeStruct((2, bsz, seq, 256), F32), jax.ShapeDtypeStruct((bsz, 2, 128, 256), F32)],
        grid_spec=grid_spec,
        compiler_params=_cparams(),
        name="ssd_scan",
    )(lidx, xbc, small, dtb_row, alog_exp, alog_small, h0)


def _gdn_kernel(l_ref, q_ref, k_ref, v_ref, sm_ref, dtb_ref, alog_ref, s0_ref, o_ref, sf_ref,
                s_ref, u_s, w_s, kt_s, qg_s, at_s, gl_s, *, nck):
    del l_ref
    cl = GDN_CHUNK
    d = pl.program_id(1)
    i = pl.program_id(2)

    @pl.when(i == 0)
    def _():
        s_ref[...] = s0_ref[...]

    fwd = d == 0
    sgn = jnp.where(fwd, 1, -1)
    ii = _iota((cl, D_GROUP), 0)
    jj = _iota((cl, D_GROUP), 1) & (cl - 1)
    dif = (ii - jj) * sgn
    low = dif >= 0
    strict = dif > 0
    diag = ii == jj
    blk16 = (ii >> 4) == (jj >> 4)
    blk32 = (ii >> 5) == (jj >> 5)
    off_masks = (blk32 & jnp.logical_not(blk16), jnp.logical_not(blk32))
    head_bd = (_iota((D_GROUP, D_GROUP), 0) >> 6) == (_iota((D_GROUP, D_GROUP), 1) >> 6)
    tri_bf = jnp.where((_iota((cl, cl), 0) - _iota((cl, cl), 1)) * sgn >= 0, 1.0, 0.0).astype(BF16)
    src = _iota((SMALL_W, D_GROUP), 0)
    dst = 4 * d + (_iota((SMALL_W, D_GROUP), 1) >> 6)
    expand_g = jnp.where(src == 8 + dst, 1.0, 0.0).astype(BF16)
    expand_b = jnp.where(src == 16 + dst, 1.0, 0.0).astype(BF16)
    a_row = -jnp.exp(alog_ref[...])

    def blockdiag(y_bf):
        return jnp.where(head_bd, jnp.concatenate([y_bf] * N_HEADS, axis=0), jnp.zeros((), BF16))

    def mul(xc, yc):
        return _dot(_bf(xc), blockdiag(_bf(yc)))

    for c in range(nck):
        rows = slice(c * cl, (c + 1) * cl)
        sm = sm_ref[rows, :]
        g_exp = _mm_xe(_softplus(sm + dtb_ref[...]), expand_g) * a_row
        b_exp = _mm_xe(_sigmoid(sm), expand_b)
        gc = _mm_ex(tri_bf, g_exp)
        gc_row = jnp.sum(jnp.where(diag, gc, 0.0), axis=0, keepdims=True)
        decay = jnp.exp(jnp.where(low, gc - gc_row, NEG_BIG))
        kc = k_ref[rows, :]
        qs = q_ref[rows, :] * (HEAD_DIM ** -0.5)
        kb = kc * b_exp
        k_bd = blockdiag(_bf(kc))
        a = jnp.where(strict, _dot_nt(_bf(kb), k_bd) * decay, 0.0)
        attn = _dot_nt(_bf(qs), k_bd) * decay
        ad = jnp.where(blk16, a, 0.0)
        a2 = mul(ad, ad)
        a4 = mul(a2, a2)
        a8 = mul(a4, a4)
        m = -ad
        for p in (a2, a4, a8):
            m = m + p + mul(m, p)
        for off in off_masks:
            o = jnp.where(off, a, 0.0)
            x = o + mul(m, o)
            m = m - (x + mul(x, m))
        rv = v_ref[rows, :] * b_exp
        rw = kb * jnp.exp(gc)
        g_last = jnp.where(fwd, gc[cl - 1:cl, :], gc[0:1, :])
        u_s[rows, :] = rv + mul(m, rv)
        w_s[rows, :] = rw + mul(m, rw)
        kt_s[rows, :] = kc * jnp.exp(g_last - gc)
        qg_s[rows, :] = qs * jnp.exp(gc)
        at_s[rows, :] = attn
        gl_s[c:c + 1, :] = g_last

    def step(j, carry):
        c = jnp.where(fwd, j, nck - 1 - j)
        st = pl.multiple_of(c * cl, cl)
        s = s_ref[...]
        s_bf = _bf(s)
        v_new = u_s[pl.ds(st, cl), :] - _dot(_bf(w_s[pl.ds(st, cl), :]), s_bf)
        v_bf = _bf(v_new)
        o_ref[pl.ds(st, cl), :] = (_dot(_bf(at_s[pl.ds(st, cl), :]), blockdiag(v_bf))
                                   + _dot(_bf(qg_s[pl.ds(st, cl), :]), s_bf))
        upd = _dot_tn(_bf(kt_s[pl.ds(st, cl), :]), v_bf)
        s_ref[...] = s * jnp.exp(gl_s[pl.ds(c, 1), :]) + jnp.where(head_bd, upd, 0.0)
        return carry

    lax.fori_loop(0, nck, step, 0)

    @pl.when(i == pl.num_programs(2) - 1)
    def _():
        sf_ref[...] = s_ref[...]


def _gdn_scan(lidx, q, k, v, small, dtb_row, alog_exp, s0):
    bsz, seq, _ = q.shape
    tb = min(seq, 512)
    nb = seq // tb
    nck = tb // GDN_CHUNK
    seq_spec, par_spec, dir_par_spec, state_spec, out_spec = _scan_specs(tb, nb)
    grid_spec = pltpu.PrefetchScalarGridSpec(
        num_scalar_prefetch=1,
        grid=(bsz, 2, nb),
        in_specs=[seq_spec(256), seq_spec(256), seq_spec(256), seq_spec(SMALL_W), par_spec((1, SMALL_W)),
                  dir_par_spec((1, 256)), state_spec(256)],
        out_specs=[out_spec, state_spec(256)],
        scratch_shapes=[pltpu.VMEM((256, 256), F32)] + [pltpu.VMEM((tb, 256), F32)] * 5
                       + [pltpu.VMEM((nck, 256), F32)],
    )
    return pl.pallas_call(
        functools.partial(_gdn_kernel, nck=nck),
        out_shape=[jax.ShapeDtypeStruct((2, bsz, seq, 256), F32), jax.ShapeDtypeStruct((bsz, 2, 256, 256), F32)],
        grid_spec=grid_spec,
        compiler_params=_cparams(),
        name="gdn_scan",
    )(lidx, q, k, v, small, dtb_row, alog_exp, s0)


def _mid_kernel(l_ref, x_ref, mod_ref, yf_ref, yb_ref, xs_ref, z_ref, of_ref, ob_ref, gate_ref, yp_ref, ym_ref,
                dsk_ref, snw_ref, gnw_ref, wo_ref, l1w_ref, l1b_ref, wup_ref, x1_ref, a_ref, b_ref):
    del l_ref
    dm = D_MODEL
    mod = mod_ref[...]
    ys = (yf_ref[...] + yb_ref[...] + xs_ref[...] * dsk_ref[...]) * _silu(z_ref[...])
    ys = ys * lax.rsqrt(jnp.mean(ys * ys, axis=-1, keepdims=True) + RMS_EPS) * snw_ref[...]
    og = of_ref[...] + ob_ref[...]
    ms = _mm_xe(og * og, _head_ones()) * (1.0 / HEAD_DIM)
    og = og * lax.rsqrt(ms + RMS_EPS) * gnw_ref[...] * _silu(gate_ref[...])
    proj = (_dot(_bf(ys), wo_ref[0:256, :]) + _dot(_bf(yp_ref[...]), wo_ref[256:512, :])
            + _dot(_bf(og), wo_ref[512:768, :]) + _dot(_bf(ym_ref[...]), wo_ref[768:1024, :]))
    x1 = _ln(DN_ALPHA * x_ref[...] + mod[:, 2 * dm:3 * dm] * proj) * l1w_ref[...] + l1b_ref[...]
    x1_ref[...] = x1
    h2 = _ln(x1) * (1.0 + mod[:, 4 * dm:5 * dm]) + mod[:, 3 * dm:4 * dm]
    ab = _dot(_bf(h2), wup_ref[...])
    a_ref[...] = ab[:, 0:D_FF]
    b_ref[...] = ab[:, D_FF:2 * D_FF]


def _mid(lidx, x, mods, y_ssd, xbc, z, o_gdn, gate, y_pool, y_gmlp, d_exp, ssd_nw, gdn_nw, w_out, ln1_w, ln1_b,
         ffn_up, is_ctx):
    bsz, seq, _ = x.shape
    t = min(seq, 256)
    dir_spec = lambda dd: pl.BlockSpec((None, None, t, 256), lambda b, i, l: (dd, b, i, 0))
    grid_spec = pltpu.PrefetchScalarGridSpec(
        num_scalar_prefetch=1,
        grid=(bsz, seq // t),
        in_specs=[_row_spec(t, D_MODEL), _mod_spec(is_ctx), dir_spec(0), dir_spec(1), _row_spec(t, 256),
                  _row_spec(t, 256), dir_spec(0), dir_spec(1), _row_spec(t, 256), _row_spec(t, 256),
                  _row_spec(t, 256), _layer_spec((1, 256)), _layer_spec((1, 256)), _layer_spec((1, 256)),
                  _layer_spec((D_MODEL, D_MODEL)), _layer_spec((1, D_MODEL)), _layer_spec((1, D_MODEL)),
                  _layer_spec((D_MODEL, 2 * D_FF))],
        out_specs=[_row_spec(t, D_MODEL), _row_spec(t, D_FF), _row_spec(t, D_FF)],
    )
    return pl.pallas_call(
        _mid_kernel,
        out_shape=[jax.ShapeDtypeStruct((bsz, seq, D_MODEL), F32), jax.ShapeDtypeStruct((bsz, seq, D_FF), F32),
                   jax.ShapeDtypeStruct((bsz, seq, D_FF), F32)],
        grid_spec=grid_spec,
        compiler_params=_cparams(),
        name="mid",
    )(lidx, x, mods, y_ssd, y_ssd, xbc, z, o_gdn, o_gdn, gate, y_pool, y_gmlp, d_exp, ssd_nw, gdn_nw, w_out,
      ln1_w, ln1_b, ffn_up)


def _ffn_kernel(l_ref, a_ref, p_ref, n_ref, b_ref, x1_ref, mod_ref, cw_ref, cb_ref, wd_ref, l2w_ref, l2b_ref,
                x2_ref):
    del l_ref
    i = pl.program_id(1)
    prev = jnp.where(i > 0, p_ref[...], 0.0)
    nxt = jnp.where(i < pl.num_programs(1) - 1, n_ref[...], 0.0)
    conv = _conv_taps(prev, a_ref[...], nxt, cw_ref[...], FFN_CONV_W) + cb_ref[...]
    f = _dot(_bf(_silu(conv) * b_ref[...]), wd_ref[...])
    g2 = mod_ref[:, 5 * D_MODEL:6 * D_MODEL]
    x2_ref[...] = _ln(DN_ALPHA * x1_ref[...] + g2 * f) * l2w_ref[...] + l2b_ref[...]


def _ffn_tail(lidx, a, b, x1, mods, conv_w, conv_b, ffn_down, ln2_w, ln2_b, is_ctx):
    bsz, seq, _ = x1.shape
    t = min(seq, 256)
    prev, nxt = _halo_specs(t, seq, D_FF)
    grid_spec = pltpu.PrefetchScalarGridSpec(
        num_scalar_prefetch=1,
        grid=(bsz, seq // t),
        in_specs=[_row_spec(t, D_FF), prev, nxt, _row_spec(t, D_FF), _row_spec(t, D_MODEL), _mod_spec(is_ctx),
                  _layer_spec((FFN_CONV_W, D_FF)), _layer_spec((1, D_FF)), _layer_spec((D_FF, D_MODEL)),
                  _layer_spec((1, D_MODEL)), _layer_spec((1, D_MODEL))],
        out_specs=_row_spec(t, D_MODEL),
    )
    return pl.pallas_call(
        _ffn_kernel,
        out_shape=jax.ShapeDtypeStruct((bsz, seq, D_MODEL), F32),
        grid_spec=grid_spec,
        compiler_params=_cparams(),
        name="ffn_tail",
    )(lidx, a, a, a, b, x1, mods, conv_w, conv_b, ffn_down, ln2_w, ln2_b)


def _prep_params(w_in, ssd_conv_w, ssd_conv_b, ssd_dt_bias, ssd_a_log, ssd_d, ssd_norm_w, pool_w, pool_scale,
                 gdn_conv_w, gdn_dt_bias, gdn_a_log, gdn_norm_w, gmlp_ln_w, gmlp_ln_b, gmlp_ws, gmlp_bs, w_out,
                 ln1_w, ln1_b, ffn_up, ffn_conv_w, ffn_conv_b, ffn_down, ln2_w, ln2_b):
    nl = DEPTH
    cols = [w_in[:, :, 0:256], w_in[:, :, 256:768], w_in[:, :, 1032:1800], w_in[:, :, 776:1032],
            w_in[:, :, 1800:2056], w_in[:, :, 2072:2584], w_in[:, :, 768:776], w_in[:, :, 2056:2072],
            jnp.zeros((nl, D_MODEL, SMALL_W - 24), w_in.dtype)]
    row = lambda t: t.reshape(nl, 1, -1)
    pad_small = lambda t: jnp.concatenate([t, jnp.zeros((nl, SMALL_W - t.shape[-1]), F32)], axis=-1).reshape(nl, 1, SMALL_W)
    eye4 = jnp.eye(4, dtype=F32)
    pool_bd = (eye4[None, :, None, :, None] * pool_w[:, :, :, None, :]).reshape(nl, 256, 256)
    return dict(
        w_in=jnp.concatenate(cols, axis=-1).astype(BF16),
        conv_w=jnp.concatenate([ssd_conv_w, gdn_conv_w], axis=-1),
        conv_b=jnp.concatenate([ssd_conv_b, jnp.zeros((nl, 768), F32)], axis=-1).reshape(nl, 1, CONV_C),
        dtb_row=pad_small(jnp.concatenate([ssd_dt_bias.reshape(nl, 8), gdn_dt_bias.reshape(nl, 8)], axis=-1)),
        ssd_alog_exp=jnp.repeat(ssd_a_log, HEAD_DIM, axis=-1).reshape(nl, 2, 1, 256),
        ssd_alog_small=pad_small(ssd_a_log.reshape(nl, 8)),
        gdn_alog_exp=jnp.repeat(gdn_a_log, HEAD_DIM, axis=-1).reshape(nl, 2, 1, 256),
        d_exp=row(jnp.repeat(ssd_d, HEAD_DIM, axis=-1)),
        ssd_nw=row(ssd_norm_w),
        gdn_nw=row(jnp.tile(gdn_norm_w, (1, N_HEADS))),
        pool_wbd=pool_bd.astype(BF16),
        pool_scale=row(pool_scale),
        gmlp_ln_w=row(gmlp_ln_w),
        gmlp_ln_b=row(gmlp_ln_b),
        gmlp_ws=gmlp_ws.astype(BF16),
        gmlp_bs=jnp.repeat(jnp.swapaxes(gmlp_bs, 1, 2), HEAD_DIM, axis=-1),
        w_out=w_out.astype(BF16),
        ln1_w=row(ln1_w), ln1_b=row(ln1_b),
        ffn_up=ffn_up.astype(BF16),
        ffn_conv_w=ffn_conv_w,
        ffn_conv_b=row(ffn_conv_b),
        ffn_down=ffn_down.astype(BF16),
        ln2_w=row(ln2_w), ln2_b=row(ln2_b),
    )


def _stream_layer(lidx, x, mods, p, states0, is_ctx, full):
    z, cv, pool_in, gate, uv, small = _in_proj(lidx, x, mods, p["w_in"], is_ctx)
    xbc, q, k, v = _conv_mix(lidx, cv, p["conv_w"], p["conv_b"])
    y_ssd, ssd_fin = _ssd_scan(lidx, xbc, small, p["dtb_row"], p["ssd_alog_exp"], p["ssd_alog_small"], states0[0])
    o_gdn, gdn_fin = _gdn_scan(lidx, q, k, v, small, p["dtb_row"], p["gdn_alog_exp"], states0[1])
    if not full:
        return None, (ssd_fin, gdn_fin)
    y_pool, y_gmlp = _pool_gmlp(lidx, pool_in, uv, p["pool_wbd"], p["pool_scale"], p["gmlp_ln_w"], p["gmlp_ln_b"],
                                p["gmlp_ws"], p["gmlp_bs"], on_grid=not is_ctx)
    x1, a, b = _mid(lidx, x, mods, y_ssd, xbc, z, o_gdn, gate, y_pool, y_gmlp, p["d_exp"], p["ssd_nw"], p["gdn_nw"],
                    p["w_out"], p["ln1_w"], p["ln1_b"], p["ffn_up"], is_ctx)
    x2 = _ffn_tail(lidx, a, b, x1, mods, p["ffn_conv_w"], p["ffn_conv_b"], p["ffn_down"], p["ln2_w"], p["ln2_b"],
                   is_ctx)
    return x2, (ssd_fin, gdn_fin)


def kernel(x, c, ctx, c_ctx, w_mod, b_mod, w_in, ssd_conv_w, ssd_conv_b, ssd_dt_bias, ssd_a_log, ssd_d, ssd_norm_w, pool_w, pool_scale, gdn_conv_w, gdn_dt_bias, gdn_a_log, gdn_norm_w, gmlp_ln_w, gmlp_ln_b, gmlp_ws, gmlp_bs, w_out, ln1_w, ln1_b, ffn_up, ffn_conv_w, ffn_conv_b, ffn_down, ln2_w, ln2_b):
    bsz = x.shape[0]
    p = _prep_params(w_in, ssd_conv_w, ssd_conv_b, ssd_dt_bias, ssd_a_log, ssd_d, ssd_norm_w, pool_w, pool_scale,
                     gdn_conv_w, gdn_dt_bias, gdn_a_log, gdn_norm_w, gmlp_ln_w, gmlp_ln_b, gmlp_ws, gmlp_bs, w_out,
                     ln1_w, ln1_b, ffn_up, ffn_conv_w, ffn_conv_b, ffn_down, ln2_w, ln2_b)
    cs = jnp.concatenate([c_ctx[None, :], c, jnp.zeros((8 - 1 - bsz, D_MODEL), F32)], axis=0)
    mods = _modulation(cs, w_mod, b_mod).reshape(DEPTH, 8, 1, 6 * D_MODEL)
    zero_states = (jnp.zeros((bsz, 2, 128, 256), F32), jnp.zeros((bsz, 2, 256, 256), F32))
    lat, cx = x, ctx
    for l in range(DEPTH):
        lidx = jnp.full((1,), l, jnp.int32)
        cx_next, ctx_states = _stream_layer(lidx, cx, mods, p, zero_states, True, l < DEPTH - 1)
        lat, _ = _stream_layer(lidx, lat, mods, p, ctx_states, False, True)
        cx = cx_next
    return lat
```

```python
import functools

import jax
import jax.numpy as jnp
from jax import lax
from jax.experimental import pallas as pl
from jax.experimental.pallas import tpu as pltpu

F32 = jnp.float32
BF16 = jnp.bfloat16

D_MODEL = 1024
DEPTH = 4
D_GROUP = 256
N_HEADS = 4
HEAD_DIM = 64
SSD_CHUNK = 128
GDN_CHUNK = 64
GMLP_CHUNK = 128
GRID_W = 64
POOL_WINDOWS = (2, 4, 8, 16)
D_FF = 2816
CONV_W = 7
FFN_CONV_W = 3
CONV_C = 1280
N_PROJ = 2688
SMALL_W = 128
HALO_F32 = 8
HALO_BF16 = 16
LN_EPS = 1e-6
RMS_EPS = 1e-6
DN_ALPHA = (2 * DEPTH) ** 0.25
NEG_BIG = -1e30
VMEM_LIMIT = 56 * 1024 * 1024


def _iota(shape, dim):
    return lax.broadcasted_iota(jnp.int32, shape, dim)


def _bf(x):
    return x.astype(BF16)


def _dot(a, b):
    return jnp.dot(a, b, preferred_element_type=F32)


def _dot_nt(a, b):
    return lax.dot_general(a, b, (((1,), (1,)), ((), ())), preferred_element_type=F32)


def _dot_tn(a, b):
    return lax.dot_general(a, b, (((0,), (0,)), ((), ())), preferred_element_type=F32)


def _split3(x):
    x1 = _bf(x)
    r1 = x - x1.astype(F32)
    x2 = _bf(r1)
    x3 = _bf(r1 - x2.astype(F32))
    return x1, x2, x3


def _mm_xe(x, e):
    x1, x2, x3 = _split3(x)
    return _dot(x1, e) + _dot(x2, e) + _dot(x3, e)


def _mm_ex(e, x):
    x1, x2, x3 = _split3(x)
    return _dot(e, x1) + _dot(e, x2) + _dot(e, x3)


def _mm3(a, b):
    ah = _bf(a)
    al = _bf(a - ah.astype(F32))
    bh = _bf(b)
    bl = _bf(b - bh.astype(F32))
    return _dot(ah, bh) + _dot(ah, bl) + _dot(al, bh)


def _sigmoid(x):
    return 0.5 * jnp.tanh(0.5 * x) + 0.5


def _silu(x):
    h = 0.5 * x
    return h + h * jnp.tanh(h)


def _softplus(x):
    return jnp.maximum(x, 0.0) + jnp.log1p(jnp.exp(-jnp.abs(x)))


def _gelu_tanh(x):
    return x * (0.5 * (1.0 + jnp.tanh(0.7978845608028654 * (x + 0.044715 * (x * x * x)))))


def _ln(x):
    mu = jnp.mean(x, axis=-1, keepdims=True)
    xc = x - mu
    var = jnp.mean(xc * xc, axis=-1, keepdims=True)
    return xc * lax.rsqrt(var + LN_EPS)


def _head_ones():
    return jnp.where((_iota((D_GROUP, D_GROUP), 0) >> 6) == (_iota((D_GROUP, D_GROUP), 1) >> 6), 1.0, 0.0).astype(BF16)


def _cparams():
    return pltpu.CompilerParams(vmem_limit_bytes=VMEM_LIMIT)


def _resident(shape, index_map):
    return pl.BlockSpec(shape, index_map, pipeline_mode=pl.Buffered(1))


def _mod_kernel(c_ref, w_ref, b_ref, o_ref):
    o_ref[...] = _mm3(_silu(c_ref[...]), w_ref[...]) + b_ref[...]


def _modulation(cs, w_mod, b_mod):
    tn = 768
    n6 = 6 * D_MODEL
    return pl.pallas_call(
        _mod_kernel,
        out_shape=jax.ShapeDtypeStruct((DEPTH, 8, n6), F32),
        grid=(DEPTH, n6 // tn),
        in_specs=[pl.BlockSpec((8, D_MODEL), lambda l, j: (0, 0)),
                  pl.BlockSpec((None, D_MODEL, tn), lambda l, j: (l, 0, j)),
                  pl.BlockSpec((None, 1, tn), lambda l, j: (l, 0, j))],
        out_specs=pl.BlockSpec((None, 8, tn), lambda l, j: (l, 0, j)),
        compiler_params=_cparams(),
        name="modulation",
    )(cs, w_mod, b_mod.reshape(DEPTH, 1, n6))


def _in_kernel(l_ref, x_ref, mod_ref, w_ref, z_ref, cv_ref, pool_ref, gate_ref, uv_ref, sm_ref):
    del l_ref
    mod = mod_ref[...]
    h = _ln(x_ref[...]) * (1.0 + mod[:, D_MODEL:2 * D_MODEL]) + mod[:, 0:D_MODEL]
    p = _dot(_bf(h), w_ref[...])
    z_ref[...] = p[:, 0:256]
    cv_ref[...] = p[:, 256:1536]
    pool_ref[...] = p[:, 1536:1792]
    gate_ref[...] = p[:, 1792:2048]
    uv_ref[...] = p[:, 2048:2560]
    sm_ref[...] = p[:, 2560:2688]


def _mod_spec(is_ctx):
    if is_ctx:
        return pl.BlockSpec((None, None, 1, 6 * D_MODEL), lambda b, i, l: (l[0], 0, 0, 0))
    return pl.BlockSpec((None, None, 1, 6 * D_MODEL), lambda b, i, l: (l[0], b + 1, 0, 0))


def _row_spec(t, width):
    return pl.BlockSpec((None, t, width), lambda b, i, l: (b, i, 0))


def _layer_spec(shape):
    nd = len(shape)
    return _resident((None,) + tuple(shape), lambda b, i, l: (l[0],) + (0,) * nd)


def _in_proj(lidx, x, mods, w_in_p, is_ctx):
    bsz, seq, _ = x.shape
    t = min(seq, 512)
    widths = (256, CONV_C, 256, 256, 512, SMALL_W)
    grid_spec = pltpu.PrefetchScalarGridSpec(
        num_scalar_prefetch=1,
        grid=(bsz, seq // t),
        in_specs=[_row_spec(t, D_MODEL), _mod_spec(is_ctx), _layer_spec((D_MODEL, N_PROJ))],
        out_specs=[_row_spec(t, w) for w in widths],
    )
    return pl.pallas_call(
        _in_kernel,
        out_shape=[jax.ShapeDtypeStruct((bsz, seq, w), F32) for w in widths],
        grid_spec=grid_spec,
        compiler_params=_cparams(),
        name="in_proj",
    )(lidx, x, mods, w_in_p)


def _conv_taps(prev, cur, nxt, w, width):
    t = cur.shape[0]
    halo = prev.shape[0]
    win = jnp.concatenate([prev, cur, nxt], axis=0)
    n = t + 2 * halo
    half = width // 2
    acc = None
    for k in range(width):
        shift = (half - k) % n
        r = pltpu.roll(win, shift, 0) if shift else win
        term = r[halo:halo + t] * w[k:k + 1, :]
        acc = term if acc is None else acc + term
    return acc


def _conv_kernel(l_ref, x_ref, p_ref, n_ref, w_ref, b_ref, xbc_ref, q_ref, k_ref, v_ref):
    del l_ref
    i = pl.program_id(1)
    has_prev = i > 0
    has_next = i < pl.num_programs(1) - 1
    ones_bd = _head_ones()
    outs = (None, None, q_ref, k_ref, v_ref)
    for s in range(CONV_C // D_GROUP):
        sl = slice(s * D_GROUP, (s + 1) * D_GROUP)
        prev = jnp.where(has_prev, p_ref[:, sl], 0.0)
        nxt = jnp.where(has_next, n_ref[:, sl], 0.0)
        y = _silu(_conv_taps(prev, x_ref[:, sl], nxt, w_ref[:, sl], CONV_W) + b_ref[:, sl])
        if s < 2:
            xbc_ref[:, sl] = y
        elif s < 4:
            ss = _mm_xe(y * y, ones_bd)
            outs[s][...] = y * lax.rsqrt(ss + 1e-6)
        else:
            outs[s][...] = y


def _halo_specs(t, seq, width, halo):
    per = t // halo
    last = seq // halo - 1
    prev = pl.BlockSpec((None, halo, width), lambda b, i, l: (b, jnp.maximum(i * per - 1, 0), 0))
    nxt = pl.BlockSpec((None, halo, width), lambda b, i, l: (b, jnp.minimum((i + 1) * per, last), 0))
    return prev, nxt


def _conv_mix(lidx, cv, conv_w, conv_b):
    bsz, seq, _ = cv.shape
    t = min(seq, 256)
    prev, nxt = _halo_specs(t, seq, CONV_C, HALO_F32)
    grid_spec = pltpu.PrefetchScalarGridSpec(
        num_scalar_prefetch=1,
        grid=(bsz, seq // t),
        in_specs=[_row_spec(t, CONV_C), prev, nxt, _layer_spec((CONV_W, CONV_C)), _layer_spec((1, CONV_C))],
        out_specs=[_row_spec(t, 512), _row_spec(t, 256), _row_spec(t, 256), _row_spec(t, 256)],
    )
    return pl.pallas_call(
        _conv_kernel,
        out_shape=[jax.ShapeDtypeStruct((bsz, seq, w), F32) for w in (512, 256, 256, 256)],
        grid_spec=grid_spec,
        compiler_params=_cparams(),
        name="conv_mix",
    )(lidx, cv, cv, cv, conv_w, conv_b)


def _pool_gmlp_kernel(l_ref, p_ref, uv_ref, pw_ref, ps_ref, lw_ref, lb_ref, ws_ref, bs_ref, yp_ref, ym_ref, *,
                      row_w):
    del l_ref
    t = p_ref.shape[0]
    x = p_ref[...]
    lane_grp = _iota((t, D_GROUP), 1) >> 6
    ii = _iota((t, t), 0)
    jj = _iota((t, t), 1)
    base = ii - (ii & (row_w - 1))
    pos = ii & (row_w - 1)
    pos_c = _iota((t, D_GROUP), 0) & (row_w - 1)
    pooled = jnp.zeros((t, D_GROUP), F32)
    for g, w in enumerate(POOL_WINDOWS):
        lo = jnp.maximum(pos - w // 2, 0)
        hi = jnp.minimum(pos + (w - w // 2), row_w)
        band = jnp.where((jj >= base + lo) & (jj < base + hi), 1.0, 0.0).astype(BF16)
        cnt = (jnp.minimum(pos_c + (w - w // 2), row_w) - jnp.maximum(pos_c - w // 2, 0)).astype(F32)
        xg = jnp.where(lane_grp == g, x, 0.0)
        pooled = pooled + _mm_ex(band, xg) / cnt
    yp_ref[...] = _dot(_bf(pooled - x), pw_ref[...]) * ps_ref[...]

    gl = _gelu_tanh(uv_ref[...])
    u = gl[:, 0:D_GROUP]
    v = _ln(gl[:, D_GROUP:2 * D_GROUP]) * lw_ref[...] + lb_ref[...]
    grp = _iota((GMLP_CHUNK, D_GROUP), 1) >> 6
    for c in range(t // GMLP_CHUNK):
        rows = slice(c * GMLP_CHUNK, (c + 1) * GMLP_CHUNK)
        vc = v[rows]
        acc = bs_ref[...]
        for g in range(4):
            acc = acc + _dot(ws_ref[g], _bf(jnp.where(grp == g, vc, 0.0)))
        ym_ref[rows, :] = u[rows] * acc


def _pool_gmlp(lidx, pool_in, uv, pool_wbd, pool_scale, ln_w, ln_b, ws, bs_exp, on_grid):
    bsz, seq, _ = pool_in.shape
    t = min(seq, 256)
    row_w = GRID_W if on_grid else seq
    assert on_grid or t == seq
    grid_spec = pltpu.PrefetchScalarGridSpec(
        num_scalar_prefetch=1,
        grid=(bsz, seq // t),
        in_specs=[_row_spec(t, 256), _row_spec(t, 512), _layer_spec((256, 256)), _layer_spec((1, 256)),
                  _layer_spec((1, 256)), _layer_spec((1, 256)), _layer_spec((4, GMLP_CHUNK, GMLP_CHUNK)),
                  _layer_spec((GMLP_CHUNK, 256))],
        out_specs=[_row_spec(t, 256), _row_spec(t, 256)],
    )
    return pl.pallas_call(
        functools.partial(_pool_gmlp_kernel, row_w=row_w),
        out_shape=[jax.ShapeDtypeStruct((bsz, seq, 256), F32)] * 2,
        grid_spec=grid_spec,
        compiler_params=_cparams(),
        name="pool_gmlp",
    )(lidx, pool_in, uv, pool_wbd, pool_scale, ln_w, ln_b, ws, bs_exp)


SCAN_GROUP = 8


def _scan_specs(tb, nb):
    fwd = lambda width: pl.BlockSpec((None, tb, width), lambda b, i, l: (b, i, 0))
    bwd = lambda width: pl.BlockSpec((None, tb, width), lambda b, i, l: (b, nb - 1 - i, 0))
    par = lambda shape: _resident((None,) + tuple(shape), lambda b, i, l: (l[0],) + (0,) * len(shape))
    state = lambda rows: pl.BlockSpec((None, 2, rows, 256), lambda b, i, l: (b, 0, 0, 0))
    return fwd, bwd, par, state


def _groups(tasks):
    return [tasks[s:s + SCAN_GROUP] for s in range(0, len(tasks), SCAN_GROUP)]


def _ssd_kernel(l_ref, xf_ref, smf_ref, xb_ref, smb_ref, dtb_ref, alog_ref, asm_ref, h0_ref, yf_ref, yb_ref, hf_ref,
                s_ref, yi_s, ec_s, upd_s, el_s, *, nc):
    del l_ref
    q = SSD_CHUNK
    i = pl.program_id(1)

    @pl.when(i == 0)
    def _():
        s_ref[...] = h0_ref[...]

    x_refs = (xf_ref, xb_ref)
    sm_refs = (smf_ref, smb_ref)
    y_refs = (yf_ref, yb_ref)
    sign = (1, -1)
    dif = _iota((q, q), 0) - _iota((q, q), 1)
    dif4 = _iota((q, 4 * q), 0) - (_iota((q, 4 * q), 1) & (q - 1))
    tri_bf = [jnp.where(dif * s >= 0, 1.0, 0.0).astype(BF16) for s in sign]
    tri4 = [dif4 * s >= 0 for s in sign]
    src = _iota((SMALL_W, D_GROUP), 0)
    expand = [jnp.where(src == 4 * d + (_iota((SMALL_W, D_GROUP), 1) >> 6), 1.0, 0.0).astype(BF16) for d in range(2)]
    src4 = _iota((SMALL_W, 4 * q), 0)
    expand4 = [jnp.where(src4 == 4 * d + (_iota((SMALL_W, 4 * q), 1) >> 7), 1.0, 0.0).astype(BF16) for d in range(2)]
    a_row = [-jnp.exp(alog_ref[d]) for d in range(2)]
    a_small = -jnp.exp(asm_ref[...])
    lane_c = _iota((q, q), 1) >> 6
    head_rows = (_iota((4 * q, D_GROUP), 0) >> 7) == (_iota((4 * q, D_GROUP), 1) >> 6)
    state_mask = (_iota((q, D_GROUP), 0) >> 6) == (_iota((q, D_GROUP), 1) >> 7)

    dt_exp, dta, dts = [], [], []
    for d in range(2):
        dt_all = _softplus(sm_refs[d][...] + dtb_ref[...])
        e = _mm_xe(dt_all, expand[d])
        dt_exp.append(e)
        dta.append(e * a_row[d])
        dts.append(dt_all * a_small)

    for tasks in _groups([(d, c) for c in range(nc) for d in range(2)]):
        rows = [slice(c * q, (c + 1) * q) for _, c in tasks]
        cum = [_mm_ex(tri_bf[d], dta[d][r]) for (d, _), r in zip(tasks, rows)]
        cums = [_mm_ex(tri_bf[d], dts[d][r]) for (d, _), r in zip(tasks, rows)]
        col4 = [_mm_xe(cs, expand4[d]) for (d, _), cs in zip(tasks, cums)]
        cum_t = [cs.T for cs in cums]
        row4 = [jnp.concatenate([ct[4 * d + h:4 * d + h + 1, :] for h in range(N_HEADS)], axis=1)
                for (d, _), ct in zip(tasks, cum_t)]
        decay = [jnp.exp(jnp.where(tri4[d], c4 - r4, NEG_BIG)) for (d, _), c4, r4 in zip(tasks, col4, row4)]
        xc = [x_refs[d][r, :] for (d, _), r in zip(tasks, rows)]
        bm_bf = [_bf(x[:, 256:384]) for x in xc]
        gram = [[_dot_nt(_bf(jnp.where(lane_c == g, x[:, 384:512], 0.0)), b) for g in range(2)]
                for x, b in zip(xc, bm_bf)]
        xdt = [x[:, 0:256] * dt_exp[d][r] for (d, _), x, r in zip(tasks, xc, rows)]
        x_bd = [jnp.where(head_rows, jnp.concatenate([_bf(v)] * N_HEADS, axis=0), jnp.zeros((), BF16)) for v in xdt]
        scores = [jnp.concatenate([g[0], g[0], g[1], g[1]], axis=1) * dc for g, dc in zip(gram, decay)]
        y_intra = [_dot(_bf(sc), xb) for sc, xb in zip(scores, x_bd)]
        last = [cm[q - 1:q, :] if d == 0 else cm[0:1, :] for (d, _), cm in zip(tasks, cum)]
        upd = [_dot_tn(b, _bf(v * jnp.exp(la - cm))) for b, v, la, cm in zip(bm_bf, xdt, last, cum)]
        for t, (d, c) in enumerate(tasks):
            yi_s[d, rows[t], :] = y_intra[t]
            ec_s[d, rows[t], :] = jnp.exp(cum[t])
            upd_s[d, c] = jnp.where(state_mask, upd[t], 0.0)
            el_s[d, c] = jnp.exp(last[t])

    for j in range(nc):
        for d in range(2):
            c = j if d == 0 else nc - 1 - j
            r = slice(c * q, (c + 1) * q)
            s = s_ref[d]
            y_refs[d][r, :] = _dot(_bf(x_refs[d][r, 384:512]), _bf(s)) * ec_s[d, r, :] + yi_s[d, r, :]
            s_ref[d] = s * el_s[d, c] + upd_s[d, c]

    @pl.when(i == pl.num_programs(1) - 1)
    def _():
        hf_ref[...] = s_ref[...]


def _ssd_scan(lidx, xbc, small, dtb_row, alog_exp, alog_small, h0):
    bsz, seq, _ = xbc.shape
    tb = min(seq, 512)
    nb = seq // tb
    nc = tb // SSD_CHUNK
    fwd, bwd, par, state = _scan_specs(tb, nb)
    grid_spec = pltpu.PrefetchScalarGridSpec(
        num_scalar_prefetch=1,
        grid=(bsz, nb),
        in_specs=[fwd(512), fwd(SMALL_W), bwd(512), bwd(SMALL_W), par((1, SMALL_W)), par((2, 1, 256)),
                  par((1, SMALL_W)), state(128)],
        out_specs=[fwd(256), bwd(256), state(128)],
        scratch_shapes=[pltpu.VMEM((2, 128, 256), F32), pltpu.VMEM((2, tb, 256), F32), pltpu.VMEM((2, tb, 256), F32),
                        pltpu.VMEM((2, nc, 128, 256), F32), pltpu.VMEM((2, nc, 1, 256), F32)],
    )
    return pl.pallas_call(
        functools.partial(_ssd_kernel, nc=nc),
        out_shape=[jax.ShapeDtypeStruct((bsz, seq, 256), F32), jax.ShapeDtypeStruct((bsz, seq, 256), F32),
                   jax.ShapeDtypeStruct((bsz, 2, 128, 256), F32)],
        grid_spec=grid_spec,
        compiler_params=_cparams(),
        name="ssd_scan",
    )(lidx, xbc, small, xbc, small, dtb_row, alog_exp, alog_small, h0)


def _gdn_kernel(l_ref, qf_ref, kf_ref, vf_ref, smf_ref, qb_ref, kb_ref, vb_ref, smb_ref, dtb_ref, alog_ref, s0_ref,
                of_ref, ob_ref, sf_ref, s_ref, u_s, kt_s, wq_s, at_s, gl_s, *, nck):
    del l_ref
    cl = GDN_CHUNK
    i = pl.program_id(1)

    @pl.when(i == 0)
    def _():
        s_ref[...] = s0_ref[...]

    q_refs = (qf_ref, qb_ref)
    k_refs = (kf_ref, kb_ref)
    v_refs = (vf_ref, vb_ref)
    sm_refs = (smf_ref, smb_ref)
    o_refs = (of_ref, ob_ref)
    sign = (1, -1)
    ii = _iota((cl, D_GROUP), 0)
    jj = _iota((cl, D_GROUP), 1) & (cl - 1)
    low = [(ii - jj) * s >= 0 for s in sign]
    strict = [(ii - jj) * s > 0 for s in sign]
    diag = ii == jj
    blk16 = (ii >> 4) == (jj >> 4)
    blk32 = (ii >> 5) == (jj >> 5)
    off_masks = (blk32 & jnp.logical_not(blk16), jnp.logical_not(blk32))
    head_bd = (_iota((D_GROUP, D_GROUP), 0) >> 6) == (_iota((D_GROUP, D_GROUP), 1) >> 6)
    dif = _iota((cl, cl), 0) - _iota((cl, cl), 1)
    tri_bf = [jnp.where(dif * s >= 0, 1.0, 0.0).astype(BF16) for s in sign]
    src = _iota((SMALL_W, 2 * D_GROUP), 0)
    lane = _iota((SMALL_W, 2 * D_GROUP), 1)
    expand = [jnp.where(src == 8 + 8 * (lane >> 8) + 4 * d + ((lane >> 6) & 3), 1.0, 0.0).astype(BF16)
              for d in range(2)]
    a_row = [-jnp.exp(alog_ref[d]) for d in range(2)]

    def blockdiag(y_bf):
        return jnp.where(head_bd, jnp.concatenate([y_bf] * N_HEADS, axis=0), jnp.zeros((), BF16))

    def mul(xs, ys):
        return [_dot(_bf(x), blockdiag(_bf(y))) for x, y in zip(xs, ys)]

    g_exp, b_exp = [], []
    for d in range(2):
        sm = sm_refs[d][...]
        lane_s = _iota(sm.shape, 1)
        t = jnp.where((lane_s >= 8) & (lane_s < 16), _softplus(sm + dtb_ref[...]), _sigmoid(sm))
        e = _mm_xe(t, expand[d])
        g_exp.append(e[:, 0:D_GROUP] * a_row[d])
        b_exp.append(e[:, D_GROUP:2 * D_GROUP])

    for tasks in _groups([(d, c) for c in range(nck) for d in range(2)]):
        ds = [d for d, _ in tasks]
        rows = [slice(c * cl, (c + 1) * cl) for _, c in tasks]
        gc = [_mm_ex(tri_bf[d], g_exp[d][r]) for d, r in zip(ds, rows)]
        gc_row = [jnp.sum(jnp.where(diag, g, 0.0), axis=0, keepdims=True) for g in gc]
        decay = [jnp.exp(jnp.where(low[d], g - gr, NEG_BIG)) for d, g, gr in zip(ds, gc, gc_row)]
        kc = [k_refs[d][r, :] for d, r in zip(ds, rows)]
        qs = [q_refs[d][r, :] * (HEAD_DIM ** -0.5) for d, r in zip(ds, rows)]
        be = [b_exp[d][r] for d, r in zip(ds, rows)]
        kb = [k * b for k, b in zip(kc, be)]
        k_bd = [blockdiag(_bf(k)) for k in kc]
        a = [jnp.where(strict[d], _dot_nt(_bf(x), kd) * dc, 0.0) for d, x, kd, dc in zip(ds, kb, k_bd, decay)]
        attn = [_dot_nt(_bf(x), kd) * dc for x, kd, dc in zip(qs, k_bd, decay)]
        ad = [jnp.where(blk16, x, 0.0) for x in a]
        a2 = mul(ad, ad)
        a4 = mul(a2, a2)
        a8 = mul(a4, a4)
        m = [-x for x in ad]
        for p in (a2, a4, a8):
            m = [mm + pp + mp for mm, pp, mp in zip(m, p, mul(m, p))]
        for off in off_masks:
            o = [jnp.where(off, x, 0.0) for x in a]
            x = [oo + mo for oo, mo in zip(o, mul(m, o))]
            m = [mm - (xx + xm) for mm, xx, xm in zip(m, x, mul(x, m))]
        rv = [v_refs[d][r, :] * b for d, r, b in zip(ds, rows, be)]
        rw = [x * jnp.exp(g) for x, g in zip(kb, gc)]
        u = [x + mx for x, mx in zip(rv, mul(m, rv))]
        w = [x + mx for x, mx in zip(rw, mul(m, rw))]
        for t, (d, c) in enumerate(tasks):
            g_last = gc[t][cl - 1:cl, :] if d == 0 else gc[t][0:1, :]
            u_s[d, rows[t], :] = u[t]
            kt_s[d, rows[t], :] = kc[t] * jnp.exp(g_last - gc[t])
            wq_s[d, c, 0:cl, :] = w[t]
            wq_s[d, c, cl:2 * cl, :] = qs[t] * jnp.exp(gc[t])
            at_s[d, rows[t], :] = attn[t]
            gl_s[d, c] = jnp.exp(g_last)

    for j in range(nck):
        cs = [j, nck - 1 - j]
        rows = [slice(c * cl, (c + 1) * cl) for c in cs]
        s = [s_ref[d] for d in range(2)]
        ws = [_dot(_bf(wq_s[d, cs[d]]), _bf(s[d])) for d in range(2)]
        v_bf = [_bf(u_s[d, rows[d], :] - ws[d][0:cl]) for d in range(2)]
        upd = [_dot_tn(_bf(kt_s[d, rows[d], :]), v_bf[d]) for d in range(2)]
        o = [_dot(_bf(at_s[d, rows[d], :]), blockdiag(v_bf[d])) + ws[d][cl:2 * cl] for d in range(2)]
        for d in range(2):
            o_refs[d][rows[d], :] = o[d]
            s_ref[d] = s[d] * gl_s[d, cs[d]] + jnp.where(head_bd, upd[d], 0.0)

    @pl.when(i == pl.num_programs(1) - 1)
    def _():
        sf_ref[...] = s_ref[...]


def _gdn_scan(lidx, q, k, v, small, dtb_row, alog_exp, s0):
    bsz, seq, _ = q.shape
    tb = min(seq, 512)
    nb = seq // tb
    nck = tb // GDN_CHUNK
    fwd, bwd, par, state = _scan_specs(tb, nb)
    grid_spec = pltpu.PrefetchScalarGridSpec(
        num_scalar_prefetch=1,
        grid=(bsz, nb),
        in_specs=[fwd(256), fwd(256), fwd(256), fwd(SMALL_W), bwd(256), bwd(256), bwd(256), bwd(SMALL_W),
                  par((1, SMALL_W)), par((2, 1, 256)), state(256)],
        out_specs=[fwd(256), bwd(256), state(256)],
        scratch_shapes=[pltpu.VMEM((2, 256, 256), F32), pltpu.VMEM((2, tb, 256), F32), pltpu.VMEM((2, tb, 256), F32),
                        pltpu.VMEM((2, nck, 2 * GDN_CHUNK, 256), F32), pltpu.VMEM((2, tb, 256), F32),
                        pltpu.VMEM((2, nck, 1, 256), F32)],
    )
    return pl.pallas_call(
        functools.partial(_gdn_kernel, nck=nck),
        out_shape=[jax.ShapeDtypeStruct((bsz, seq, 256), F32), jax.ShapeDtypeStruct((bsz, seq, 256), F32),
                   jax.ShapeDtypeStruct((bsz, 2, 256, 256), F32)],
        grid_spec=grid_spec,
        compiler_params=_cparams(),
        name="gdn_scan",
    )(lidx, q, k, v, small, q, k, v, small, dtb_row, alog_exp, s0)


def _mid_kernel(l_ref, x_ref, mod_ref, yf_ref, yb_ref, xs_ref, z_ref, of_ref, ob_ref, gate_ref, yp_ref, ym_ref,
                dsk_ref, snw_ref, gnw_ref, wo_ref, l1w_ref, l1b_ref, wup_ref, x1_ref, a_ref, b_ref):
    del l_ref
    dm = D_MODEL
    mod = mod_ref[...]
    ys = (yf_ref[...] + yb_ref[...] + xs_ref[...] * dsk_ref[...]) * _silu(z_ref[...])
    ys = ys * lax.rsqrt(jnp.mean(ys * ys, axis=-1, keepdims=True) + RMS_EPS) * snw_ref[...]
    og = of_ref[...] + ob_ref[...]
    ms = _mm_xe(og * og, _head_ones()) * (1.0 / HEAD_DIM)
    og = og * lax.rsqrt(ms + RMS_EPS) * gnw_ref[...] * _silu(gate_ref[...])
    proj = (_dot(_bf(ys), wo_ref[0:256, :]) + _dot(_bf(yp_ref[...]), wo_ref[256:512, :])
            + _dot(_bf(og), wo_ref[512:768, :]) + _dot(_bf(ym_ref[...]), wo_ref[768:1024, :]))
    x1 = _ln(DN_ALPHA * x_ref[...] + mod[:, 2 * dm:3 * dm] * proj) * l1w_ref[...] + l1b_ref[...]
    x1_ref[...] = x1
    h2 = _ln(x1) * (1.0 + mod[:, 4 * dm:5 * dm]) + mod[:, 3 * dm:4 * dm]
    ab = _dot(_bf(h2), wup_ref[...])
    a_ref[...] = _bf(ab[:, 0:D_FF])
    b_ref[...] = _bf(ab[:, D_FF:2 * D_FF])


def _mid(lidx, x, mods, y_f, y_b, xbc, z, o_f, o_b, gate, y_pool, y_gmlp, d_exp, ssd_nw, gdn_nw, w_out, ln1_w, ln1_b,
         ffn_up, is_ctx):
    bsz, seq, _ = x.shape
    t = min(seq, 256)
    grid_spec = pltpu.PrefetchScalarGridSpec(
        num_scalar_prefetch=1,
        grid=(bsz, seq // t),
        in_specs=[_row_spec(t, D_MODEL), _mod_spec(is_ctx)] + [_row_spec(t, 256)] * 9
                 + [_layer_spec((1, 256)), _layer_spec((1, 256)), _layer_spec((1, 256)),
                    _layer_spec((D_MODEL, D_MODEL)), _layer_spec((1, D_MODEL)), _layer_spec((1, D_MODEL)),
                    _layer_spec((D_MODEL, 2 * D_FF))],
        out_specs=[_row_spec(t, D_MODEL), _row_spec(t, D_FF), _row_spec(t, D_FF)],
    )
    return pl.pallas_call(
        _mid_kernel,
        out_shape=[jax.ShapeDtypeStruct((bsz, seq, D_MODEL), F32), jax.ShapeDtypeStruct((bsz, seq, D_FF), BF16),
                   jax.ShapeDtypeStruct((bsz, seq, D_FF), BF16)],
        grid_spec=grid_spec,
        compiler_params=_cparams(),
        name="mid",
    )(lidx, x, mods, y_f, y_b, xbc, z, o_f, o_b, gate, y_pool, y_gmlp, d_exp, ssd_nw, gdn_nw, w_out,
      ln1_w, ln1_b, ffn_up)


def _ffn_kernel(l_ref, a_ref, p_ref, n_ref, b_ref, x1_ref, mod_ref, cw_ref, cb_ref, wd_ref, l2w_ref, l2b_ref,
                x2_ref):
    del l_ref
    i = pl.program_id(1)
    prev = jnp.where(i > 0, p_ref[...].astype(F32), 0.0)
    nxt = jnp.where(i < pl.num_programs(1) - 1, n_ref[...].astype(F32), 0.0)
    conv = _conv_taps(prev, a_ref[...].astype(F32), nxt, cw_ref[...], FFN_CONV_W) + cb_ref[...]
    f = _dot(_bf(_silu(conv) * b_ref[...].astype(F32)), wd_ref[...])
    g2 = mod_ref[:, 5 * D_MODEL:6 * D_MODEL]
    x2_ref[...] = _ln(DN_ALPHA * x1_ref[...] + g2 * f) * l2w_ref[...] + l2b_ref[...]


def _ffn_tail(lidx, a, b, x1, mods, conv_w, conv_b, ffn_down, ln2_w, ln2_b, is_ctx):
    bsz, seq, _ = x1.shape
    t = min(seq, 256)
    prev, nxt = _halo_specs(t, seq, D_FF, HALO_BF16)
    grid_spec = pltpu.PrefetchScalarGridSpec(
        num_scalar_prefetch=1,
        grid=(bsz, seq // t),
        in_specs=[_row_spec(t, D_FF), prev, nxt, _row_spec(t, D_FF), _row_spec(t, D_MODEL), _mod_spec(is_ctx),
                  _layer_spec((FFN_CONV_W, D_FF)), _layer_spec((1, D_FF)), _layer_spec((D_FF, D_MODEL)),
                  _layer_spec((1, D_MODEL)), _layer_spec((1, D_MODEL))],
        out_specs=_row_spec(t, D_MODEL),
    )
    return pl.pallas_call(
        _ffn_kernel,
        out_shape=jax.ShapeDtypeStruct((bsz, seq, D_MODEL), F32),
        grid_spec=grid_spec,
        compiler_params=_cparams(),
        name="ffn_tail",
    )(lidx, a, a, a, b, x1, mods, conv_w, conv_b, ffn_down, ln2_w, ln2_b)


def _prep_params(w_in, ssd_conv_w, ssd_conv_b, ssd_dt_bias, ssd_a_log, ssd_d, ssd_norm_w, pool_w, pool_scale,
                 gdn_conv_w, gdn_dt_bias, gdn_a_log, gdn_norm_w, gmlp_ln_w, gmlp_ln_b, gmlp_ws, gmlp_bs, w_out,
                 ln1_w, ln1_b, ffn_up, ffn_conv_w, ffn_conv_b, ffn_down, ln2_w, ln2_b):
    nl = DEPTH
    cols = [w_in[:, :, 0:256], w_in[:, :, 256:768], w_in[:, :, 1032:1800], w_in[:, :, 776:1032],
            w_in[:, :, 1800:2056], w_in[:, :, 2072:2584], w_in[:, :, 768:776], w_in[:, :, 2056:2072],
            jnp.zeros((nl, D_MODEL, SMALL_W - 24), w_in.dtype)]
    row = lambda t: t.reshape(nl, 1, -1)
    pad_small = lambda t: jnp.concatenate([t, jnp.zeros((nl, SMALL_W - t.shape[-1]), F32)], axis=-1).reshape(nl, 1, SMALL_W)
    eye4 = jnp.eye(4, dtype=F32)
    pool_bd = (eye4[None, :, None, :, None] * pool_w[:, :, :, None, :]).reshape(nl, 256, 256)
    return dict(
        w_in=jnp.concatenate(cols, axis=-1).astype(BF16),
        conv_w=jnp.concatenate([ssd_conv_w, gdn_conv_w], axis=-1),
        conv_b=jnp.concatenate([ssd_conv_b, jnp.zeros((nl, 768), F32)], axis=-1).reshape(nl, 1, CONV_C),
        dtb_row=pad_small(jnp.concatenate([ssd_dt_bias.reshape(nl, 8), gdn_dt_bias.reshape(nl, 8)], axis=-1)),
        ssd_alog_exp=jnp.repeat(ssd_a_log, HEAD_DIM, axis=-1).reshape(nl, 2, 1, 256),
        ssd_alog_small=pad_small(ssd_a_log.reshape(nl, 8)),
        gdn_alog_exp=jnp.repeat(gdn_a_log, HEAD_DIM, axis=-1).reshape(nl, 2, 1, 256),
        d_exp=row(jnp.repeat(ssd_d, HEAD_DIM, axis=-1)),
        ssd_nw=row(ssd_norm_w),
        gdn_nw=row(jnp.tile(gdn_norm_w, (1, N_HEADS))),
        pool_wbd=pool_bd.astype(BF16),
        pool_scale=row(pool_scale),
        gmlp_ln_w=row(gmlp_ln_w),
        gmlp_ln_b=row(gmlp_ln_b),
        gmlp_ws=gmlp_ws.astype(BF16),
        gmlp_bs=jnp.repeat(jnp.swapaxes(gmlp_bs, 1, 2), HEAD_DIM, axis=-1),
        w_out=w_out.astype(BF16),
        ln1_w=row(ln1_w), ln1_b=row(ln1_b),
        ffn_up=ffn_up.astype(BF16),
        ffn_conv_w=ffn_conv_w,
        ffn_conv_b=row(ffn_conv_b),
        ffn_down=ffn_down.astype(BF16),
        ln2_w=row(ln2_w), ln2_b=row(ln2_b),
    )


def _stream_layer(lidx, x, mods, p, states0, is_ctx, full):
    z, cv, pool_in, gate, uv, small = _in_proj(lidx, x, mods, p["w_in"], is_ctx)
    xbc, q, k, v = _conv_mix(lidx, cv, p["conv_w"], p["conv_b"])
    y_f, y_b, ssd_fin = _ssd_scan(lidx, xbc, small, p["dtb_row"], p["ssd_alog_exp"], p["ssd_alog_small"], states0[0])
    o_f, o_b, gdn_fin = _gdn_scan(lidx, q, k, v, small, p["dtb_row"], p["gdn_alog_exp"], states0[1])
    if not full:
        return None, (ssd_fin, gdn_fin)
    y_pool, y_gmlp = _pool_gmlp(lidx, pool_in, uv, p["pool_wbd"], p["pool_scale"], p["gmlp_ln_w"], p["gmlp_ln_b"],
                                p["gmlp_ws"], p["gmlp_bs"], on_grid=not is_ctx)
    x1, a, b = _mid(lidx, x, mods, y_f, y_b, xbc, z, o_f, o_b, gate, y_pool, y_gmlp, p["d_exp"], p["ssd_nw"], p["gdn_nw"],
                    p["w_out"], p["ln1_w"], p["ln1_b"], p["ffn_up"], is_ctx)
    x2 = _ffn_tail(lidx, a, b, x1, mods, p["ffn_conv_w"], p["ffn_conv_b"], p["ffn_down"], p["ln2_w"], p["ln2_b"],
                   is_ctx)
    return x2, (ssd_fin, gdn_fin)


def kernel(x, c, ctx, c_ctx, w_mod, b_mod, w_in, ssd_conv_w, ssd_conv_b, ssd_dt_bias, ssd_a_log, ssd_d, ssd_norm_w, pool_w, pool_scale, gdn_conv_w, gdn_dt_bias, gdn_a_log, gdn_norm_w, gmlp_ln_w, gmlp_ln_b, gmlp_ws, gmlp_bs, w_out, ln1_w, ln1_b, ffn_up, ffn_conv_w, ffn_conv_b, ffn_down, ln2_w, ln2_b):
    bsz = x.shape[0]
    p = _prep_params(w_in, ssd_conv_w, ssd_conv_b, ssd_dt_bias, ssd_a_log, ssd_d, ssd_norm_w, pool_w, pool_scale,
                     gdn_conv_w, gdn_dt_bias, gdn_a_log, gdn_norm_w, gmlp_ln_w, gmlp_ln_b, gmlp_ws, gmlp_bs, w_out,
                     ln1_w, ln1_b, ffn_up, ffn_conv_w, ffn_conv_b, ffn_down, ln2_w, ln2_b)
    cs = jnp.concatenate([c_ctx[None, :], c, jnp.zeros((8 - 1 - bsz, D_MODEL), F32)], axis=0)
    mods = _modulation(cs, w_mod, b_mod).reshape(DEPTH, 8, 1, 6 * D_MODEL)
    zero_states = (jnp.zeros((bsz, 2, 128, 256), F32), jnp.zeros((bsz, 2, 256, 256), F32))
    lat, cx = x, ctx
    for l in range(DEPTH):
        lidx = jnp.full((1,), l, jnp.int32)
        cx_next, ctx_states = _stream_layer(lidx, cx, mods, p, zero_states, True, l < DEPTH - 1)
        lat, _ = _stream_layer(lidx, lat, mods, p, ctx_states, False, True)
        cx = cx_next
    return lat
```

```python
import functools

import jax
import jax.numpy as jnp
from jax import lax
from jax.experimental import pallas as pl
from jax.experimental.pallas import tpu as pltpu

F32 = jnp.float32
BF16 = jnp.bfloat16

D_MODEL = 1024
DEPTH = 4
D_GROUP = 256
N_HEADS = 4
HEAD_DIM = 64
SSD_CHUNK = 128
GDN_CHUNK = 64
GMLP_CHUNK = 128
GRID_W = 64
POOL_WINDOWS = (2, 4, 8, 16)
D_FF = 2816
CONV_W = 7
FFN_CONV_W = 3
CONV_C = 1280
N_PROJ = 2688
SMALL_W = 128
HALO_F32 = 8
HALO_BF16 = 16
LN_EPS = 1e-6
RMS_EPS = 1e-6
DN_ALPHA = (2 * DEPTH) ** 0.25
NEG_BIG = -1e30
VMEM_LIMIT = 56 * 1024 * 1024


def _iota(shape, dim):
    return lax.broadcasted_iota(jnp.int32, shape, dim)


def _bf(x):
    return x.astype(BF16)


def _dot(a, b):
    return jnp.dot(a, b, preferred_element_type=F32)


def _dot_nt(a, b):
    return lax.dot_general(a, b, (((1,), (1,)), ((), ())), preferred_element_type=F32)


def _dot_tn(a, b):
    return lax.dot_general(a, b, (((0,), (0,)), ((), ())), preferred_element_type=F32)


def _split3(x):
    x1 = _bf(x)
    r1 = x - x1.astype(F32)
    x2 = _bf(r1)
    x3 = _bf(r1 - x2.astype(F32))
    return x1, x2, x3


def _mm_xe(x, e):
    x1, x2, x3 = _split3(x)
    return _dot(x1, e) + _dot(x2, e) + _dot(x3, e)


def _mm_ex(e, x):
    x1, x2, x3 = _split3(x)
    return _dot(e, x1) + _dot(e, x2) + _dot(e, x3)


def _mm3(a, b):
    ah = _bf(a)
    al = _bf(a - ah.astype(F32))
    bh = _bf(b)
    bl = _bf(b - bh.astype(F32))
    return _dot(ah, bh) + _dot(ah, bl) + _dot(al, bh)


def _sigmoid(x):
    return 0.5 * jnp.tanh(0.5 * x) + 0.5


def _silu(x):
    h = 0.5 * x
    return h + h * jnp.tanh(h)


def _softplus(x):
    return jnp.maximum(x, 0.0) + jnp.log1p(jnp.exp(-jnp.abs(x)))


def _gelu_tanh(x):
    return x * (0.5 * (1.0 + jnp.tanh(0.7978845608028654 * (x + 0.044715 * (x * x * x)))))


def _ln(x):
    mu = jnp.mean(x, axis=-1, keepdims=True)
    xc = x - mu
    var = jnp.mean(xc * xc, axis=-1, keepdims=True)
    return xc * lax.rsqrt(var + LN_EPS)


def _head_ones():
    return jnp.where((_iota((D_GROUP, D_GROUP), 0) >> 6) == (_iota((D_GROUP, D_GROUP), 1) >> 6), 1.0, 0.0).astype(BF16)


def _cparams():
    return pltpu.CompilerParams(vmem_limit_bytes=VMEM_LIMIT)


def _resident(shape, index_map):
    return pl.BlockSpec(shape, index_map, pipeline_mode=pl.Buffered(1))


def _mod_kernel(c_ref, w_ref, b_ref, o_ref):
    o_ref[...] = _mm3(_silu(c_ref[...]), w_ref[...]) + b_ref[...]


def _modulation(cs, w_mod, b_mod):
    tn = 768
    n6 = 6 * D_MODEL
    return pl.pallas_call(
        _mod_kernel,
        out_shape=jax.ShapeDtypeStruct((DEPTH, 8, n6), F32),
        grid=(DEPTH, n6 // tn),
        in_specs=[pl.BlockSpec((8, D_MODEL), lambda l, j: (0, 0)),
                  pl.BlockSpec((None, D_MODEL, tn), lambda l, j: (l, 0, j)),
                  pl.BlockSpec((None, 1, tn), lambda l, j: (l, 0, j))],
        out_specs=pl.BlockSpec((None, 8, tn), lambda l, j: (l, 0, j)),
        compiler_params=_cparams(),
        name="modulation",
    )(cs, w_mod, b_mod.reshape(DEPTH, 1, n6))


ROW_TILE = 256


def _mod_spec(is_ctx):
    if is_ctx:
        return pl.BlockSpec((None, None, 1, 6 * D_MODEL), lambda b, i, l: (l[0], 0, 0, 0))
    return pl.BlockSpec((None, None, 1, 6 * D_MODEL), lambda b, i, l: (l[0], b + 1, 0, 0))


def _row_spec(t, width):
    return pl.BlockSpec((None, t, width), lambda b, i, l: (b, i, 0))


def _layer_spec(shape):
    nd = len(shape)
    return _resident((None,) + tuple(shape), lambda b, i, l: (l[0],) + (0,) * nd)


def _conv_window(win, w, width, halo, t):
    n = win.shape[0]
    half = width // 2
    acc = None
    for k in range(width):
        shift = (half - k) % n
        r = pltpu.roll(win, shift, 0) if shift else win
        term = r[halo:halo + t] * w[k:k + 1, :]
        acc = term if acc is None else acc + term
    return acc


def _pool_gmlp(x, uv, pw_ref, ps_ref, lw_ref, lb_ref, ws_ref, bs_ref, yp_ref, ym_ref, row_w):
    t = x.shape[0]
    lane_grp = _iota((t, D_GROUP), 1) >> 6
    ii = _iota((t, t), 0)
    jj = _iota((t, t), 1)
    base = ii - (ii & (row_w - 1))
    pos = ii & (row_w - 1)
    pos_c = _iota((t, D_GROUP), 0) & (row_w - 1)
    pooled = jnp.zeros((t, D_GROUP), F32)
    x_terms = _split3(x)
    for g, w in enumerate(POOL_WINDOWS):
        lo = jnp.maximum(pos - w // 2, 0)
        hi = jnp.minimum(pos + (w - w // 2), row_w)
        band = jnp.where((jj >= base + lo) & (jj < base + hi), 1.0, 0.0).astype(BF16)
        cnt = (jnp.minimum(pos_c + (w - w // 2), row_w) - jnp.maximum(pos_c - w // 2, 0)).astype(F32)
        in_grp = lane_grp == g
        window_sum = sum(_dot(band, jnp.where(in_grp, xt, jnp.zeros((), BF16))) for xt in x_terms)
        pooled = pooled + window_sum / cnt
    yp_ref[...] = _dot(_bf(pooled - x), pw_ref[...]) * ps_ref[...]

    gl = _gelu_tanh(uv)
    u = gl[:, 0:D_GROUP]
    v = _ln(gl[:, D_GROUP:2 * D_GROUP]) * lw_ref[...] + lb_ref[...]
    grp = _iota((GMLP_CHUNK, D_GROUP), 1) >> 6
    for c in range(t // GMLP_CHUNK):
        rows = slice(c * GMLP_CHUNK, (c + 1) * GMLP_CHUNK)
        vc = v[rows]
        acc = bs_ref[...]
        for g in range(4):
            acc = acc + _dot(ws_ref[g], _bf(jnp.where(grp == g, vc, 0.0)))
        ym_ref[rows, :] = u[rows] * acc


def _front_kernel(l_ref, x_ref, mod_ref, w_ref, cw_ref, cb_ref, pw_ref, ps_ref, lw_ref, lb_ref, ws_ref, bs_ref,
                  z_ref, gate_ref, sm_ref, yp_ref, ym_ref, xbc_ref, q_ref, k_ref, v_ref, win_s, *, row_w):
    del l_ref
    t = x_ref.shape[0]
    halo = HALO_F32
    i = pl.program_id(1)
    n = pl.num_programs(1) - 1

    @pl.when(i == 0)
    def _():
        win_s[...] = jnp.zeros(win_s.shape, F32)

    mod = mod_ref[...]
    h = _ln(x_ref[...]) * (1.0 + mod[:, D_MODEL:2 * D_MODEL]) + mod[:, 0:D_MODEL]
    p = _dot(_bf(h), w_ref[...])
    z_ref[...] = p[:, 0:256]
    gate_ref[...] = p[:, 1792:2048]
    sm_ref[...] = p[:, 2560:2688]
    _pool_gmlp(p[:, 1536:1792], p[:, 2048:2560], pw_ref, ps_ref, lw_ref, lb_ref, ws_ref, bs_ref, yp_ref, ym_ref, row_w)

    cv = p[:, 256:256 + CONV_C]
    win_s[halo + t:2 * halo + t, :] = jnp.where(i < n, cv[0:halo], 0.0)
    ones_bd = _head_ones()
    outs = (None, None, q_ref, k_ref, v_ref)
    for s in range(CONV_C // D_GROUP):
        sl = slice(s * D_GROUP, (s + 1) * D_GROUP)
        y = _silu(_conv_window(win_s[:, sl], cw_ref[:, sl], CONV_W, halo, t) + cb_ref[:, sl])
        if s < 2:
            xbc_ref[:, sl] = y
        elif s < 4:
            ss = _mm_xe(y * y, ones_bd)
            outs[s][...] = y * lax.rsqrt(ss + 1e-6)
        else:
            outs[s][...] = y
    win_s[0:halo, :] = win_s[t:t + halo, :]
    win_s[halo:halo + t, :] = cv


def _front(lidx, x, mods, p, is_ctx):
    bsz, seq, _ = x.shape
    t = min(seq, ROW_TILE)
    n = seq // t
    row_w = seq if is_ctx else GRID_W
    assert not is_ctx or t == seq
    cur = lambda width: pl.BlockSpec((None, t, width), lambda b, i, l: (b, jnp.minimum(i, n - 1), 0))
    lag = lambda width: pl.BlockSpec((None, t, width), lambda b, i, l: (b, jnp.maximum(i - 1, 0), 0))
    grid_spec = pltpu.PrefetchScalarGridSpec(
        num_scalar_prefetch=1,
        grid=(bsz, n + 1),
        in_specs=[cur(D_MODEL), _mod_spec(is_ctx), _layer_spec((D_MODEL, N_PROJ)), _layer_spec((CONV_W, CONV_C)),
                  _layer_spec((1, CONV_C)), _layer_spec((256, 256)), _layer_spec((1, 256)), _layer_spec((1, 256)),
                  _layer_spec((1, 256)), _layer_spec((4, GMLP_CHUNK, GMLP_CHUNK)), _layer_spec((GMLP_CHUNK, 256))],
        out_specs=[cur(256), cur(256), cur(SMALL_W), cur(256), cur(256), lag(512), lag(256), lag(256), lag(256)],
        scratch_shapes=[pltpu.VMEM((t + 2 * HALO_F32, CONV_C), F32)],
    )
    widths = (256, 256, SMALL_W, 256, 256, 512, 256, 256, 256)
    return pl.pallas_call(
        functools.partial(_front_kernel, row_w=row_w),
        out_shape=[jax.ShapeDtypeStruct((bsz, seq, w), F32) for w in widths],
        grid_spec=grid_spec,
        compiler_params=_cparams(),
        name="front",
    )(lidx, x, mods, p["w_in"], p["conv_w"], p["conv_b"], p["pool_wbd"], p["pool_scale"], p["gmlp_ln_w"],
      p["gmlp_ln_b"], p["gmlp_ws"], p["gmlp_bs"])


SCAN_GROUP = 8


def _scan_specs(tb, nb):
    fwd = lambda width: pl.BlockSpec((None, tb, width), lambda b, i, l: (b, i, 0))
    bwd = lambda width: pl.BlockSpec((None, tb, width), lambda b, i, l: (b, nb - 1 - i, 0))
    par = lambda shape: _resident((None,) + tuple(shape), lambda b, i, l: (l[0],) + (0,) * len(shape))
    state = lambda rows: pl.BlockSpec((None, 2, rows, 256), lambda b, i, l: (b, 0, 0, 0))
    return fwd, bwd, par, state


def _groups(tasks):
    return [tasks[s:s + SCAN_GROUP] for s in range(0, len(tasks), SCAN_GROUP)]


def _ssd_kernel(l_ref, xf_ref, smf_ref, xb_ref, smb_ref, dtb_ref, alog_ref, asm_ref, h0_ref, yf_ref, yb_ref, hf_ref,
                s_ref, yi_s, ec_s, upd_s, el_s, *, nc):
    del l_ref
    q = SSD_CHUNK
    i = pl.program_id(1)

    @pl.when(i == 0)
    def _():
        s_ref[...] = h0_ref[...]

    x_refs = (xf_ref, xb_ref)
    sm_refs = (smf_ref, smb_ref)
    y_refs = (yf_ref, yb_ref)
    sign = (1, -1)
    dif = _iota((q, q), 0) - _iota((q, q), 1)
    dif4 = _iota((q, 4 * q), 0) - (_iota((q, 4 * q), 1) & (q - 1))
    tri_bf = [jnp.where(dif * s >= 0, 1.0, 0.0).astype(BF16) for s in sign]
    tri4 = [dif4 * s >= 0 for s in sign]
    src = _iota((SMALL_W, D_GROUP), 0)
    expand = [jnp.where(src == 4 * d + (_iota((SMALL_W, D_GROUP), 1) >> 6), 1.0, 0.0).astype(BF16) for d in range(2)]
    src4 = _iota((SMALL_W, 4 * q), 0)
    expand4 = [jnp.where(src4 == 4 * d + (_iota((SMALL_W, 4 * q), 1) >> 7), 1.0, 0.0).astype(BF16) for d in range(2)]
    a_row = [-jnp.exp(alog_ref[d]) for d in range(2)]
    a_small = -jnp.exp(asm_ref[...])
    lane_c = _iota((q, q), 1) >> 6
    head_rows = (_iota((4 * q, D_GROUP), 0) >> 7) == (_iota((4 * q, D_GROUP), 1) >> 6)
    state_mask = (_iota((q, D_GROUP), 0) >> 6) == (_iota((q, D_GROUP), 1) >> 7)

    dt_exp, dta, dts = [], [], []
    for d in range(2):
        dt_all = _softplus(sm_refs[d][...] + dtb_ref[...])
        e = _mm_xe(dt_all, expand[d])
        dt_exp.append(e)
        dta.append(e * a_row[d])
        dts.append(dt_all * a_small)

    for tasks in _groups([(d, c) for c in range(nc) for d in range(2)]):
        rows = [slice(c * q, (c + 1) * q) for _, c in tasks]
        cum = [_mm_ex(tri_bf[d], dta[d][r]) for (d, _), r in zip(tasks, rows)]
        cums = [_mm_ex(tri_bf[d], dts[d][r]) for (d, _), r in zip(tasks, rows)]
        col4 = [_mm_xe(cs, expand4[d]) for (d, _), cs in zip(tasks, cums)]
        cum_t = [cs.T for cs in cums]
        row4 = [jnp.concatenate([ct[4 * d + h:4 * d + h + 1, :] for h in range(N_HEADS)], axis=1)
                for (d, _), ct in zip(tasks, cum_t)]
        decay = [jnp.exp(jnp.where(tri4[d], c4 - r4, NEG_BIG)) for (d, _), c4, r4 in zip(tasks, col4, row4)]
        xc = [x_refs[d][r, :] for (d, _), r in zip(tasks, rows)]
        bm_bf = [_bf(x[:, 256:384]) for x in xc]
        gram = [[_dot_nt(_bf(jnp.where(lane_c == g, x[:, 384:512], 0.0)), b) for g in range(2)]
                for x, b in zip(xc, bm_bf)]
        xdt = [x[:, 0:256] * dt_exp[d][r] for (d, _), x, r in zip(tasks, xc, rows)]
        x_bd = [jnp.where(head_rows, jnp.concatenate([_bf(v)] * N_HEADS, axis=0), jnp.zeros((), BF16)) for v in xdt]
        scores = [jnp.concatenate([g[0], g[0], g[1], g[1]], axis=1) * dc for g, dc in zip(gram, decay)]
        y_intra = [_dot(_bf(sc), xb) for sc, xb in zip(scores, x_bd)]
        last = [cm[q - 1:q, :] if d == 0 else cm[0:1, :] for (d, _), cm in zip(tasks, cum)]
        upd = [_dot_tn(b, _bf(v * jnp.exp(la - cm))) for b, v, la, cm in zip(bm_bf, xdt, last, cum)]
        for t, (d, c) in enumerate(tasks):
            yi_s[d, rows[t], :] = y_intra[t]
            ec_s[d, rows[t], :] = jnp.exp(cum[t])
            upd_s[d, c] = jnp.where(state_mask, upd[t], 0.0)
            el_s[d, c] = jnp.exp(last[t])

    for j in range(nc):
        for d in range(2):
            c = j if d == 0 else nc - 1 - j
            r = slice(c * q, (c + 1) * q)
            s = s_ref[d]
            y_refs[d][r, :] = _dot(_bf(x_refs[d][r, 384:512]), _bf(s)) * ec_s[d, r, :] + yi_s[d, r, :]
            s_ref[d] = s * el_s[d, c] + upd_s[d, c]

    @pl.when(i == pl.num_programs(1) - 1)
    def _():
        hf_ref[...] = s_ref[...]


def _ssd_scan(lidx, xbc, small, dtb_row, alog_exp, alog_small, h0):
    bsz, seq, _ = xbc.shape
    tb = min(seq, 512)
    nb = seq // tb
    nc = tb // SSD_CHUNK
    fwd, bwd, par, state = _scan_specs(tb, nb)
    grid_spec = pltpu.PrefetchScalarGridSpec(
        num_scalar_prefetch=1,
        grid=(bsz, nb),
        in_specs=[fwd(512), fwd(SMALL_W), bwd(512), bwd(SMALL_W), par((1, SMALL_W)), par((2, 1, 256)),
                  par((1, SMALL_W)), state(128)],
        out_specs=[fwd(256), bwd(256), state(128)],
        scratch_shapes=[pltpu.VMEM((2, 128, 256), F32), pltpu.VMEM((2, tb, 256), F32), pltpu.VMEM((2, tb, 256), F32),
                        pltpu.VMEM((2, nc, 128, 256), F32), pltpu.VMEM((2, nc, 1, 256), F32)],
    )
    return pl.pallas_call(
        functools.partial(_ssd_kernel, nc=nc),
        out_shape=[jax.ShapeDtypeStruct((bsz, seq, 256), F32), jax.ShapeDtypeStruct((bsz, seq, 256), F32),
                   jax.ShapeDtypeStruct((bsz, 2, 128, 256), F32)],
        grid_spec=grid_spec,
        compiler_params=_cparams(),
        name="ssd_scan",
    )(lidx, xbc, small, xbc, small, dtb_row, alog_exp, alog_small, h0)


def _gdn_kernel(l_ref, qf_ref, kf_ref, vf_ref, smf_ref, qb_ref, kb_ref, vb_ref, smb_ref, dtb_ref, alog_ref, s0_ref,
                of_ref, ob_ref, sf_ref, s_ref, u_s, kt_s, wq_s, at_s, gl_s, *, nck):
    del l_ref
    cl = GDN_CHUNK
    i = pl.program_id(1)

    @pl.when(i == 0)
    def _():
        s_ref[...] = s0_ref[...]

    q_refs = (qf_ref, qb_ref)
    k_refs = (kf_ref, kb_ref)
    v_refs = (vf_ref, vb_ref)
    sm_refs = (smf_ref, smb_ref)
    o_refs = (of_ref, ob_ref)
    sign = (1, -1)
    ii = _iota((cl, D_GROUP), 0)
    jj = _iota((cl, D_GROUP), 1) & (cl - 1)
    low = [(ii - jj) * s >= 0 for s in sign]
    strict = [(ii - jj) * s > 0 for s in sign]
    diag = ii == jj
    blk16 = (ii >> 4) == (jj >> 4)
    blk32 = (ii >> 5) == (jj >> 5)
    off_masks = (blk32 & jnp.logical_not(blk16), jnp.logical_not(blk32))
    head_bd = (_iota((D_GROUP, D_GROUP), 0) >> 6) == (_iota((D_GROUP, D_GROUP), 1) >> 6)
    dif = _iota((cl, cl), 0) - _iota((cl, cl), 1)
    tri_bf = [jnp.where(dif * s >= 0, 1.0, 0.0).astype(BF16) for s in sign]
    src = _iota((SMALL_W, 2 * D_GROUP), 0)
    lane = _iota((SMALL_W, 2 * D_GROUP), 1)
    expand = [jnp.where(src == 8 + 8 * (lane >> 8) + 4 * d + ((lane >> 6) & 3), 1.0, 0.0).astype(BF16)
              for d in range(2)]
    a_row = [-jnp.exp(alog_ref[d]) for d in range(2)]

    def blockdiag(y_bf):
        return jnp.where(head_bd, jnp.concatenate([y_bf] * N_HEADS, axis=0), jnp.zeros((), BF16))

    def mul(xs, ys):
        return [_dot(_bf(x), blockdiag(_bf(y))) for x, y in zip(xs, ys)]

    g_exp, b_exp = [], []
    for d in range(2):
        sm = sm_refs[d][...]
        lane_s = _iota(sm.shape, 1)
        t = jnp.where((lane_s >= 8) & (lane_s < 16), _softplus(sm + dtb_ref[...]), _sigmoid(sm))
        e = _mm_xe(t, expand[d])
        g_exp.append(e[:, 0:D_GROUP] * a_row[d])
        b_exp.append(e[:, D_GROUP:2 * D_GROUP])

    for tasks in _groups([(d, c) for c in range(nck) for d in range(2)]):
        ds = [d for d, _ in tasks]
        rows = [slice(c * cl, (c + 1) * cl) for _, c in tasks]
        gc = [_mm_ex(tri_bf[d], g_exp[d][r]) for d, r in zip(ds, rows)]
        gc_row = [jnp.sum(jnp.where(diag, g, 0.0), axis=0, keepdims=True) for g in gc]
        decay = [jnp.exp(jnp.where(low[d], g - gr, NEG_BIG)) for d, g, gr in zip(ds, gc, gc_row)]
        kc = [k_refs[d][r, :] for d, r in zip(ds, rows)]
        qs = [q_refs[d][r, :] * (HEAD_DIM ** -0.5) for d, r in zip(ds, rows)]
        be = [b_exp[d][r] for d, r in zip(ds, rows)]
        kb = [k * b for k, b in zip(kc, be)]
        k_bd = [blockdiag(_bf(k)) for k in kc]
        kq = [_dot_nt(_bf(jnp.concatenate([x, y], axis=0)), kd) for x, y, kd in zip(kb, qs, k_bd)]
        a = [jnp.where(strict[d], x[0:cl] * dc, 0.0) for d, x, dc in zip(ds, kq, decay)]
        attn = [x[cl:2 * cl] * dc for x, dc in zip(kq, decay)]
        ad = [jnp.where(blk16, x, 0.0) for x in a]
        a2 = mul(ad, ad)
        a4 = mul(a2, a2)
        a8 = mul(a4, a4)
        m = [-x for x in ad]
        for p in (a2, a4, a8):
            m = [mm + pp + mp for mm, pp, mp in zip(m, p, mul(m, p))]
        for off in off_masks:
            o = [jnp.where(off, x, 0.0) for x in a]
            x = [oo + mo for oo, mo in zip(o, mul(m, o))]
            m = [mm - (xx + xm) for mm, xx, xm in zip(m, x, mul(x, m))]
        rv = [v_refs[d][r, :] * b for d, r, b in zip(ds, rows, be)]
        rw = [x * jnp.exp(g) for x, g in zip(kb, gc)]
        u = [x + mx for x, mx in zip(rv, mul(m, rv))]
        w = [x + mx for x, mx in zip(rw, mul(m, rw))]
        for t, (d, c) in enumerate(tasks):
            g_last = gc[t][cl - 1:cl, :] if d == 0 else gc[t][0:1, :]
            u_s[d, rows[t], :] = u[t]
            kt_s[d, rows[t], :] = kc[t] * jnp.exp(g_last - gc[t])
            wq_s[d, c, 0:cl, :] = w[t]
            wq_s[d, c, cl:2 * cl, :] = qs[t] * jnp.exp(gc[t])
            at_s[d, rows[t], :] = attn[t]
            gl_s[d, c] = jnp.exp(g_last)

    for j in range(nck):
        cs = [j, nck - 1 - j]
        rows = [slice(c * cl, (c + 1) * cl) for c in cs]
        s = [s_ref[d] for d in range(2)]
        ws = [_dot(_bf(wq_s[d, cs[d]]), _bf(s[d])) for d in range(2)]
        v_bf = [_bf(u_s[d, rows[d], :] - ws[d][0:cl]) for d in range(2)]
        upd = [_dot_tn(_bf(kt_s[d, rows[d], :]), v_bf[d]) for d in range(2)]
        o = [_dot(_bf(at_s[d, rows[d], :]), blockdiag(v_bf[d])) + ws[d][cl:2 * cl] for d in range(2)]
        for d in range(2):
            o_refs[d][rows[d], :] = o[d]
            s_ref[d] = s[d] * gl_s[d, cs[d]] + jnp.where(head_bd, upd[d], 0.0)

    @pl.when(i == pl.num_programs(1) - 1)
    def _():
        sf_ref[...] = s_ref[...]


def _gdn_scan(lidx, q, k, v, small, dtb_row, alog_exp, s0):
    bsz, seq, _ = q.shape
    tb = min(seq, 512)
    nb = seq // tb
    nck = tb // GDN_CHUNK
    fwd, bwd, par, state = _scan_specs(tb, nb)
    grid_spec = pltpu.PrefetchScalarGridSpec(
        num_scalar_prefetch=1,
        grid=(bsz, nb),
        in_specs=[fwd(256), fwd(256), fwd(256), fwd(SMALL_W), bwd(256), bwd(256), bwd(256), bwd(SMALL_W),
                  par((1, SMALL_W)), par((2, 1, 256)), state(256)],
        out_specs=[fwd(256), bwd(256), state(256)],
        scratch_shapes=[pltpu.VMEM((2, 256, 256), F32), pltpu.VMEM((2, tb, 256), F32), pltpu.VMEM((2, tb, 256), F32),
                        pltpu.VMEM((2, nck, 2 * GDN_CHUNK, 256), F32), pltpu.VMEM((2, tb, 256), F32),
                        pltpu.VMEM((2, nck, 1, 256), F32)],
    )
    return pl.pallas_call(
        functools.partial(_gdn_kernel, nck=nck),
        out_shape=[jax.ShapeDtypeStruct((bsz, seq, 256), F32), jax.ShapeDtypeStruct((bsz, seq, 256), F32),
                   jax.ShapeDtypeStruct((bsz, 2, 256, 256), F32)],
        grid_spec=grid_spec,
        compiler_params=_cparams(),
        name="gdn_scan",
    )(lidx, q, k, v, small, q, k, v, small, dtb_row, alog_exp, s0)


def _mid_kernel(l_ref, x_ref, mod_ref, yf_ref, yb_ref, xs_ref, z_ref, of_ref, ob_ref, gate_ref, yp_ref, ym_ref,
                dsk_ref, snw_ref, gnw_ref, wo_ref, l1w_ref, l1b_ref, wup_ref, x1_ref, a_ref, b_ref):
    del l_ref
    dm = D_MODEL
    mod = mod_ref[...]
    ys = (yf_ref[...] + yb_ref[...] + xs_ref[...] * dsk_ref[...]) * _silu(z_ref[...])
    ys = ys * lax.rsqrt(jnp.mean(ys * ys, axis=-1, keepdims=True) + RMS_EPS) * snw_ref[...]
    og = of_ref[...] + ob_ref[...]
    ms = _mm_xe(og * og, _head_ones()) * (1.0 / HEAD_DIM)
    og = og * lax.rsqrt(ms + RMS_EPS) * gnw_ref[...] * _silu(gate_ref[...])
    proj = (_dot(_bf(yp_ref[...]), wo_ref[256:512, :]) + _dot(_bf(ym_ref[...]), wo_ref[768:1024, :])
            + _dot(_bf(ys), wo_ref[0:256, :]) + _dot(_bf(og), wo_ref[512:768, :]))
    x1 = _ln(DN_ALPHA * x_ref[...] + mod[:, 2 * dm:3 * dm] * proj) * l1w_ref[...] + l1b_ref[...]
    x1_ref[...] = x1
    h2 = _ln(x1) * (1.0 + mod[:, 4 * dm:5 * dm]) + mod[:, 3 * dm:4 * dm]
    ab = _dot(_bf(h2), wup_ref[...])
    a_ref[...] = _bf(ab[:, 0:D_FF])
    b_ref[...] = _bf(ab[:, D_FF:2 * D_FF])


def _mid(lidx, x, mods, y_f, y_b, xbc, z, o_f, o_b, gate, y_pool, y_gmlp, d_exp, ssd_nw, gdn_nw, w_out, ln1_w, ln1_b,
         ffn_up, is_ctx):
    bsz, seq, _ = x.shape
    t = min(seq, ROW_TILE)
    grid_spec = pltpu.PrefetchScalarGridSpec(
        num_scalar_prefetch=1,
        grid=(bsz, seq // t),
        in_specs=[_row_spec(t, D_MODEL), _mod_spec(is_ctx)] + [_row_spec(t, 256)] * 9
                 + [_layer_spec((1, 256)), _layer_spec((1, 256)), _layer_spec((1, 256)),
                    _layer_spec((D_MODEL, D_MODEL)), _layer_spec((1, D_MODEL)), _layer_spec((1, D_MODEL)),
                    _layer_spec((D_MODEL, 2 * D_FF))],
        out_specs=[_row_spec(t, D_MODEL), _row_spec(t, D_FF), _row_spec(t, D_FF)],
    )
    return pl.pallas_call(
        _mid_kernel,
        out_shape=[jax.ShapeDtypeStruct((bsz, seq, D_MODEL), F32), jax.ShapeDtypeStruct((bsz, seq, D_FF), BF16),
                   jax.ShapeDtypeStruct((bsz, seq, D_FF), BF16)],
        grid_spec=grid_spec,
        compiler_params=_cparams(),
        name="mid",
    )(lidx, x, mods, y_f, y_b, xbc, z, o_f, o_b, gate, y_pool, y_gmlp, d_exp, ssd_nw, gdn_nw, w_out,
      ln1_w, ln1_b, ffn_up)


def _ffn_kernel(l_ref, a_ref, p_ref, n_ref, b_ref, x1_ref, mod_ref, cw_ref, cb_ref, wd_ref, l2w_ref, l2b_ref,
                x2_ref):
    del l_ref
    i = pl.program_id(1)
    prev = jnp.where(i > 0, p_ref[...].astype(F32), 0.0)
    nxt = jnp.where(i < pl.num_programs(1) - 1, n_ref[...].astype(F32), 0.0)
    win = jnp.concatenate([prev, a_ref[...].astype(F32), nxt], axis=0)
    conv = _conv_window(win, cw_ref[...], FFN_CONV_W, HALO_BF16, a_ref.shape[0]) + cb_ref[...]
    f = _dot(_bf(_silu(conv) * b_ref[...].astype(F32)), wd_ref[...])
    g2 = mod_ref[:, 5 * D_MODEL:6 * D_MODEL]
    x2_ref[...] = _ln(DN_ALPHA * x1_ref[...] + g2 * f) * l2w_ref[...] + l2b_ref[...]


def _halo_specs(t, seq, width, halo):
    per = t // halo
    last = seq // halo - 1
    prev = pl.BlockSpec((None, halo, width), lambda b, i, l: (b, jnp.maximum(i * per - 1, 0), 0))
    nxt = pl.BlockSpec((None, halo, width), lambda b, i, l: (b, jnp.minimum((i + 1) * per, last), 0))
    return prev, nxt


def _ffn_tail(lidx, a, b, x1, mods, conv_w, conv_b, ffn_down, ln2_w, ln2_b, is_ctx):
    bsz, seq, _ = x1.shape
    t = min(seq, ROW_TILE)
    prev, nxt = _halo_specs(t, seq, D_FF, HALO_BF16)
    grid_spec = pltpu.PrefetchScalarGridSpec(
        num_scalar_prefetch=1,
        grid=(bsz, seq // t),
        in_specs=[_row_spec(t, D_FF), prev, nxt, _row_spec(t, D_FF), _row_spec(t, D_MODEL), _mod_spec(is_ctx),
                  _layer_spec((FFN_CONV_W, D_FF)), _layer_spec((1, D_FF)), _layer_spec((D_FF, D_MODEL)),
                  _layer_spec((1, D_MODEL)), _layer_spec((1, D_MODEL))],
        out_specs=_row_spec(t, D_MODEL),
    )
    return pl.pallas_call(
        _ffn_kernel,
        out_shape=jax.ShapeDtypeStruct((bsz, seq, D_MODEL), F32),
        grid_spec=grid_spec,
        compiler_params=_cparams(),
        name="ffn_tail",
    )(lidx, a, a, a, b, x1, mods, conv_w, conv_b, ffn_down, ln2_w, ln2_b)


def _prep_params(w_in, ssd_conv_w, ssd_conv_b, ssd_dt_bias, ssd_a_log, ssd_d, ssd_norm_w, pool_w, pool_scale,
                 gdn_conv_w, gdn_dt_bias, gdn_a_log, gdn_norm_w, gmlp_ln_w, gmlp_ln_b, gmlp_ws, gmlp_bs, w_out,
                 ln1_w, ln1_b, ffn_up, ffn_conv_w, ffn_conv_b, ffn_down, ln2_w, ln2_b):
    nl = DEPTH
    cols = [w_in[:, :, 0:256], w_in[:, :, 256:768], w_in[:, :, 1032:1800], w_in[:, :, 776:1032],
            w_in[:, :, 1800:2056], w_in[:, :, 2072:2584], w_in[:, :, 768:776], w_in[:, :, 2056:2072],
            jnp.zeros((nl, D_MODEL, SMALL_W - 24), w_in.dtype)]
    row = lambda t: t.reshape(nl, 1, -1)
    pad_small = lambda t: jnp.concatenate([t, jnp.zeros((nl, SMALL_W - t.shape[-1]), F32)], axis=-1).reshape(nl, 1, SMALL_W)
    eye4 = jnp.eye(4, dtype=F32)
    pool_bd = (eye4[None, :, None, :, None] * pool_w[:, :, :, None, :]).reshape(nl, 256, 256)
    return dict(
        w_in=jnp.concatenate(cols, axis=-1).astype(BF16),
        conv_w=jnp.concatenate([ssd_conv_w, gdn_conv_w], axis=-1),
        conv_b=jnp.concatenate([ssd_conv_b, jnp.zeros((nl, 768), F32)], axis=-1).reshape(nl, 1, CONV_C),
        dtb_row=pad_small(jnp.concatenate([ssd_dt_bias.reshape(nl, 8), gdn_dt_bias.reshape(nl, 8)], axis=-1)),
        ssd_alog_exp=jnp.repeat(ssd_a_log, HEAD_DIM, axis=-1).reshape(nl, 2, 1, 256),
        ssd_alog_small=pad_small(ssd_a_log.reshape(nl, 8)),
        gdn_alog_exp=jnp.repeat(gdn_a_log, HEAD_DIM, axis=-1).reshape(nl, 2, 1, 256),
        d_exp=row(jnp.repeat(ssd_d, HEAD_DIM, axis=-1)),
        ssd_nw=row(ssd_norm_w),
        gdn_nw=row(jnp.tile(gdn_norm_w, (1, N_HEADS))),
        pool_wbd=pool_bd.astype(BF16),
        pool_scale=row(pool_scale),
        gmlp_ln_w=row(gmlp_ln_w),
        gmlp_ln_b=row(gmlp_ln_b),
        gmlp_ws=gmlp_ws.astype(BF16),
        gmlp_bs=jnp.repeat(jnp.swapaxes(gmlp_bs, 1, 2), HEAD_DIM, axis=-1),
        w_out=w_out.astype(BF16),
        ln1_w=row(ln1_w), ln1_b=row(ln1_b),
        ffn_up=ffn_up.astype(BF16),
        ffn_conv_w=ffn_conv_w,
        ffn_conv_b=row(ffn_conv_b),
        ffn_down=ffn_down.astype(BF16),
        ln2_w=row(ln2_w), ln2_b=row(ln2_b),
    )


def _stream_layer(lidx, x, mods, p, states0, is_ctx, full):
    z, gate, small, y_pool, y_gmlp, xbc, q, k, v = _front(lidx, x, mods, p, is_ctx)
    y_f, y_b, ssd_fin = _ssd_scan(lidx, xbc, small, p["dtb_row"], p["ssd_alog_exp"], p["ssd_alog_small"], states0[0])
    o_f, o_b, gdn_fin = _gdn_scan(lidx, q, k, v, small, p["dtb_row"], p["gdn_alog_exp"], states0[1])
    if not full:
        return None, (ssd_fin, gdn_fin)
    x1, a, b = _mid(lidx, x, mods, y_f, y_b, xbc, z, o_f, o_b, gate, y_pool, y_gmlp, p["d_exp"], p["ssd_nw"], p["gdn_nw"],
                    p["w_out"], p["ln1_w"], p["ln1_b"], p["ffn_up"], is_ctx)
    x2 = _ffn_tail(lidx, a, b, x1, mods, p["ffn_conv_w"], p["ffn_conv_b"], p["ffn_down"], p["ln2_w"], p["ln2_b"],
                   is_ctx)
    return x2, (ssd_fin, gdn_fin)


def kernel(x, c, ctx, c_ctx, w_mod, b_mod, w_in, ssd_conv_w, ssd_conv_b, ssd_dt_bias, ssd_a_log, ssd_d, ssd_norm_w, pool_w, pool_scale, gdn_conv_w, gdn_dt_bias, gdn_a_log, gdn_norm_w, gmlp_ln_w, gmlp_ln_b, gmlp_ws, gmlp_bs, w_out, ln1_w, ln1_b, ffn_up, ffn_conv_w, ffn_conv_b, ffn_down, ln2_w, ln2_b):
    bsz = x.shape[0]
    p = _prep_params(w_in, ssd_conv_w, ssd_conv_b, ssd_dt_bias, ssd_a_log, ssd_d, ssd_norm_w, pool_w, pool_scale,
                     gdn_conv_w, gdn_dt_bias, gdn_a_log, gdn_norm_w, gmlp_ln_w, gmlp_ln_b, gmlp_ws, gmlp_bs, w_out,
                     ln1_w, ln1_b, ffn_up, ffn_conv_w, ffn_conv_b, ffn_down, ln2_w, ln2_b)
    cs = jnp.concatenate([c_ctx[None, :], c, jnp.zeros((8 - 1 - bsz, D_MODEL), F32)], axis=0)
    mods = _modulation(cs, w_mod, b_mod).reshape(DEPTH, 8, 1, 6 * D_MODEL)
    zero_states = (jnp.zeros((bsz, 2, 128, 256), F32), jnp.zeros((bsz, 2, 256, 256), F32))
    lat, cx = x, ctx
    for l in range(DEPTH):
        lidx = jnp.full((1,), l, jnp.int32)
        cx_next, ctx_states = _stream_layer(lidx, cx, mods, p, zero_states, True, l < DEPTH - 1)
        lat, _ = _stream_layer(lidx, lat, mods, p, ctx_states, False, True)
        cx = cx_next
    return lat
```

```python
import functools

import jax
import jax.numpy as jnp
from jax import lax
from jax.experimental import pallas as pl
from jax.experimental.pallas import tpu as pltpu

F32 = jnp.float32
BF16 = jnp.bfloat16

D_MODEL = 1024
DEPTH = 4
D_GROUP = 256
N_HEADS = 4
HEAD_DIM = 64
SSD_CHUNK = 128
GDN_CHUNK = 64
GMLP_CHUNK = 128
GRID_W = 64
POOL_WINDOWS = (2, 4, 8, 16)
D_FF = 2816
CONV_W = 7
FFN_CONV_W = 3
CONV_C = 1280
N_PROJ = 2688
SMALL_W = 128
HALO_F32 = 8
LN_EPS = 1e-6
RMS_EPS = 1e-6
DN_ALPHA = (2 * DEPTH) ** 0.25
NEG_BIG = -1e30
VMEM_LIMIT = 56 * 1024 * 1024


def _iota(shape, dim):
    return lax.broadcasted_iota(jnp.int32, shape, dim)


def _bf(x):
    return x.astype(BF16)


def _dot(a, b):
    return jnp.dot(a, b, preferred_element_type=F32)


def _dot_nt(a, b):
    return lax.dot_general(a, b, (((1,), (1,)), ((), ())), preferred_element_type=F32)


def _dot_tn(a, b):
    return lax.dot_general(a, b, (((0,), (0,)), ((), ())), preferred_element_type=F32)


def _split3(x):
    x1 = _bf(x)
    r1 = x - x1.astype(F32)
    x2 = _bf(r1)
    x3 = _bf(r1 - x2.astype(F32))
    return x1, x2, x3


def _mm_xe(x, e):
    x1, x2, x3 = _split3(x)
    return _dot(x1, e) + _dot(x2, e) + _dot(x3, e)


def _mm_ex(e, x):
    x1, x2, x3 = _split3(x)
    return _dot(e, x1) + _dot(e, x2) + _dot(e, x3)


def _mm3(a, b):
    ah = _bf(a)
    al = _bf(a - ah.astype(F32))
    bh = _bf(b)
    bl = _bf(b - bh.astype(F32))
    return _dot(ah, bh) + _dot(ah, bl) + _dot(al, bh)


def _sigmoid(x):
    return 0.5 * jnp.tanh(0.5 * x) + 0.5


def _silu(x):
    h = 0.5 * x
    return h + h * jnp.tanh(h)


def _softplus(x):
    return jnp.maximum(x, 0.0) + jnp.log1p(jnp.exp(-jnp.abs(x)))


def _gelu_tanh(x):
    return x * (0.5 * (1.0 + jnp.tanh(0.7978845608028654 * (x + 0.044715 * (x * x * x)))))


def _ln(x):
    mu = jnp.mean(x, axis=-1, keepdims=True)
    xc = x - mu
    var = jnp.mean(xc * xc, axis=-1, keepdims=True)
    return xc * lax.rsqrt(var + LN_EPS)


def _head_ones():
    return jnp.where((_iota((D_GROUP, D_GROUP), 0) >> 6) == (_iota((D_GROUP, D_GROUP), 1) >> 6), 1.0, 0.0).astype(BF16)


def _cparams():
    return pltpu.CompilerParams(vmem_limit_bytes=VMEM_LIMIT)


def _resident(shape, index_map):
    return pl.BlockSpec(shape, index_map, pipeline_mode=pl.Buffered(1))


def _mod_kernel(c_ref, w_ref, b_ref, o_ref):
    o_ref[...] = _mm3(_silu(c_ref[...]), w_ref[...]) + b_ref[...]


def _modulation(cs, w_mod, b_mod):
    tn = 768
    n6 = 6 * D_MODEL
    return pl.pallas_call(
        _mod_kernel,
        out_shape=jax.ShapeDtypeStruct((DEPTH, 8, n6), F32),
        grid=(DEPTH, n6 // tn),
        in_specs=[pl.BlockSpec((8, D_MODEL), lambda l, j: (0, 0)),
                  pl.BlockSpec((None, D_MODEL, tn), lambda l, j: (l, 0, j)),
                  pl.BlockSpec((None, 1, tn), lambda l, j: (l, 0, j))],
        out_specs=pl.BlockSpec((None, 8, tn), lambda l, j: (l, 0, j)),
        compiler_params=_cparams(),
        name="modulation",
    )(cs, w_mod, b_mod.reshape(DEPTH, 1, n6))


ROW_TILE = 256


def _mod_spec(is_ctx):
    if is_ctx:
        return pl.BlockSpec((None, None, 1, 6 * D_MODEL), lambda b, i, l: (l[0], 0, 0, 0))
    return pl.BlockSpec((None, None, 1, 6 * D_MODEL), lambda b, i, l: (l[0], b + 1, 0, 0))


def _layer_spec(shape):
    nd = len(shape)
    return _resident((None,) + tuple(shape), lambda b, i, l: (l[0],) + (0,) * nd)


def _conv_window(win, w, width, halo, t):
    n = win.shape[0]
    half = width // 2
    acc = None
    for k in range(width):
        shift = (half - k) % n
        r = pltpu.roll(win, shift, 0) if shift else win
        term = r[halo:halo + t] * w[k:k + 1, :]
        acc = term if acc is None else acc + term
    return acc


def _pool_gmlp(x, uv, pw_ref, ps_ref, lw_ref, lb_ref, ws_ref, bs_ref, yp_ref, ym_ref, row_w):
    t = x.shape[0]
    lane_grp = _iota((t, D_GROUP), 1) >> 6
    ii = _iota((t, t), 0)
    jj = _iota((t, t), 1)
    base = ii - (ii & (row_w - 1))
    pos = ii & (row_w - 1)
    pos_c = _iota((t, D_GROUP), 0) & (row_w - 1)
    pooled = jnp.zeros((t, D_GROUP), F32)
    x_terms = _split3(x)
    for g, w in enumerate(POOL_WINDOWS):
        lo = jnp.maximum(pos - w // 2, 0)
        hi = jnp.minimum(pos + (w - w // 2), row_w)
        band = jnp.where((jj >= base + lo) & (jj < base + hi), 1.0, 0.0).astype(BF16)
        cnt = (jnp.minimum(pos_c + (w - w // 2), row_w) - jnp.maximum(pos_c - w // 2, 0)).astype(F32)
        in_grp = lane_grp == g
        window_sum = sum(_dot(band, jnp.where(in_grp, xt, jnp.zeros((), BF16))) for xt in x_terms)
        pooled = pooled + window_sum / cnt
    yp_ref[...] = _dot(_bf(pooled - x), pw_ref[...]) * ps_ref[...]

    gl = _gelu_tanh(uv)
    u = gl[:, 0:D_GROUP]
    v = _ln(gl[:, D_GROUP:2 * D_GROUP]) * lw_ref[...] + lb_ref[...]
    grp = _iota((GMLP_CHUNK, D_GROUP), 1) >> 6
    for c in range(t // GMLP_CHUNK):
        rows = slice(c * GMLP_CHUNK, (c + 1) * GMLP_CHUNK)
        vc = v[rows]
        acc = bs_ref[...]
        for g in range(4):
            acc = acc + _dot(ws_ref[g], _bf(jnp.where(grp == g, vc, 0.0)))
        ym_ref[rows, :] = u[rows] * acc


def _front_kernel(l_ref, x_ref, mod_ref, w_ref, cw_ref, cb_ref, pw_ref, ps_ref, lw_ref, lb_ref, ws_ref, bs_ref,
                  z_ref, gate_ref, sm_ref, yp_ref, ym_ref, xbc_ref, q_ref, k_ref, v_ref, win_s, *, row_w):
    del l_ref
    t = x_ref.shape[0]
    halo = HALO_F32
    i = pl.program_id(1)
    n = pl.num_programs(1) - 1

    @pl.when(i == 0)
    def _():
        win_s[...] = jnp.zeros(win_s.shape, F32)

    mod = mod_ref[...]
    h = _ln(x_ref[...]) * (1.0 + mod[:, D_MODEL:2 * D_MODEL]) + mod[:, 0:D_MODEL]
    p = _dot(_bf(h), w_ref[...])
    z_ref[...] = p[:, 0:256]
    gate_ref[...] = p[:, 1792:2048]
    sm_ref[...] = p[:, 2560:2688]
    _pool_gmlp(p[:, 1536:1792], p[:, 2048:2560], pw_ref, ps_ref, lw_ref, lb_ref, ws_ref, bs_ref, yp_ref, ym_ref, row_w)

    cv = p[:, 256:256 + CONV_C]
    win_s[halo + t:2 * halo + t, :] = jnp.where(i < n, cv[0:halo], 0.0)
    ones_bd = _head_ones()
    outs = (None, None, q_ref, k_ref, v_ref)
    for s in range(CONV_C // D_GROUP):
        sl = slice(s * D_GROUP, (s + 1) * D_GROUP)
        y = _silu(_conv_window(win_s[:, sl], cw_ref[:, sl], CONV_W, halo, t) + cb_ref[:, sl])
        if s < 2:
            xbc_ref[:, sl] = y
        elif s < 4:
            ss = _mm_xe(y * y, ones_bd)
            outs[s][...] = y * lax.rsqrt(ss + 1e-6)
        else:
            outs[s][...] = y
    win_s[0:halo, :] = win_s[t:t + halo, :]
    win_s[halo:halo + t, :] = cv


def _front(lidx, x, mods, p, is_ctx):
    bsz, seq, _ = x.shape
    t = min(seq, ROW_TILE)
    n = seq // t
    row_w = seq if is_ctx else GRID_W
    assert not is_ctx or t == seq
    cur = lambda width: pl.BlockSpec((None, t, width), lambda b, i, l: (b, jnp.minimum(i, n - 1), 0))
    lag = lambda width: pl.BlockSpec((None, t, width), lambda b, i, l: (b, jnp.maximum(i - 1, 0), 0))
    grid_spec = pltpu.PrefetchScalarGridSpec(
        num_scalar_prefetch=1,
        grid=(bsz, n + 1),
        in_specs=[cur(D_MODEL), _mod_spec(is_ctx), _layer_spec((D_MODEL, N_PROJ)), _layer_spec((CONV_W, CONV_C)),
                  _layer_spec((1, CONV_C)), _layer_spec((256, 256)), _layer_spec((1, 256)), _layer_spec((1, 256)),
                  _layer_spec((1, 256)), _layer_spec((4, GMLP_CHUNK, GMLP_CHUNK)), _layer_spec((GMLP_CHUNK, 256))],
        out_specs=[cur(256), cur(256), cur(SMALL_W), cur(256), cur(256), lag(512), lag(256), lag(256), lag(256)],
        scratch_shapes=[pltpu.VMEM((t + 2 * HALO_F32, CONV_C), F32)],
    )
    widths = (256, 256, SMALL_W, 256, 256, 512, 256, 256, 256)
    return pl.pallas_call(
        functools.partial(_front_kernel, row_w=row_w),
        out_shape=[jax.ShapeDtypeStruct((bsz, seq, w), F32) for w in widths],
        grid_spec=grid_spec,
        compiler_params=_cparams(),
        name="front",
    )(lidx, x, mods, p["w_in"], p["conv_w"], p["conv_b"], p["pool_wbd"], p["pool_scale"], p["gmlp_ln_w"],
      p["gmlp_ln_b"], p["gmlp_ws"], p["gmlp_bs"])


SCAN_GROUP = 8


def _scan_specs(tb, nb):
    fwd = lambda width: pl.BlockSpec((None, tb, width), lambda b, i, l: (b, i, 0))
    bwd = lambda width: pl.BlockSpec((None, tb, width), lambda b, i, l: (b, nb - 1 - i, 0))
    par = lambda shape: _resident((None,) + tuple(shape), lambda b, i, l: (l[0],) + (0,) * len(shape))
    state = lambda rows: pl.BlockSpec((None, 2, rows, 256), lambda b, i, l: (b, 0, 0, 0))
    return fwd, bwd, par, state


def _groups(tasks):
    return [tasks[s:s + SCAN_GROUP] for s in range(0, len(tasks), SCAN_GROUP)]


def _ssd_kernel(l_ref, xf_ref, smf_ref, xb_ref, smb_ref, dtb_ref, alog_ref, asm_ref, h0_ref, yf_ref, yb_ref, hf_ref,
                s_ref, yi_s, ec_s, upd_s, el_s, *, nc):
    del l_ref
    q = SSD_CHUNK
    i = pl.program_id(1)

    @pl.when(i == 0)
    def _():
        s_ref[...] = h0_ref[...]

    x_refs = (xf_ref, xb_ref)
    sm_refs = (smf_ref, smb_ref)
    y_refs = (yf_ref, yb_ref)
    sign = (1, -1)
    dif = _iota((q, q), 0) - _iota((q, q), 1)
    dif4 = _iota((q, 4 * q), 0) - (_iota((q, 4 * q), 1) & (q - 1))
    tri_bf = [jnp.where(dif * s >= 0, 1.0, 0.0).astype(BF16) for s in sign]
    tri4 = [dif4 * s >= 0 for s in sign]
    src = _iota((SMALL_W, D_GROUP), 0)
    expand = [jnp.where(src == 4 * d + (_iota((SMALL_W, D_GROUP), 1) >> 6), 1.0, 0.0).astype(BF16) for d in range(2)]
    src4 = _iota((SMALL_W, 4 * q), 0)
    expand4 = [jnp.where(src4 == 4 * d + (_iota((SMALL_W, 4 * q), 1) >> 7), 1.0, 0.0).astype(BF16) for d in range(2)]
    a_row = [-jnp.exp(alog_ref[d]) for d in range(2)]
    a_small = -jnp.exp(asm_ref[...])
    lane_c = _iota((q, q), 1) >> 6
    head_rows = (_iota((4 * q, D_GROUP), 0) >> 7) == (_iota((4 * q, D_GROUP), 1) >> 6)
    state_mask = (_iota((q, D_GROUP), 0) >> 6) == (_iota((q, D_GROUP), 1) >> 7)

    dt_exp, dta, dts = [], [], []
    for d in range(2):
        dt_all = _softplus(sm_refs[d][...] + dtb_ref[...])
        e = _mm_xe(dt_all, expand[d])
        dt_exp.append(e)
        dta.append(e * a_row[d])
        dts.append(dt_all * a_small)

    for tasks in _groups([(d, c) for c in range(nc) for d in range(2)]):
        rows = [slice(c * q, (c + 1) * q) for _, c in tasks]
        cum = [_mm_ex(tri_bf[d], dta[d][r]) for (d, _), r in zip(tasks, rows)]
        cums = [_mm_ex(tri_bf[d], dts[d][r]) for (d, _), r in zip(tasks, rows)]
        col4 = [_mm_xe(cs, expand4[d]) for (d, _), cs in zip(tasks, cums)]
        cum_t = [cs.T for cs in cums]
        row4 = [jnp.concatenate([ct[4 * d + h:4 * d + h + 1, :] for h in range(N_HEADS)], axis=1)
                for (d, _), ct in zip(tasks, cum_t)]
        decay = [jnp.exp(jnp.where(tri4[d], c4 - r4, NEG_BIG)) for (d, _), c4, r4 in zip(tasks, col4, row4)]
        xc = [x_refs[d][r, :] for (d, _), r in zip(tasks, rows)]
        bm_bf = [_bf(x[:, 256:384]) for x in xc]
        gram = [[_dot_nt(_bf(jnp.where(lane_c == g, x[:, 384:512], 0.0)), b) for g in range(2)]
                for x, b in zip(xc, bm_bf)]
        xdt = [x[:, 0:256] * dt_exp[d][r] for (d, _), x, r in zip(tasks, xc, rows)]
        x_bd = [jnp.where(head_rows, jnp.concatenate([_bf(v)] * N_HEADS, axis=0), jnp.zeros((), BF16)) for v in xdt]
        scores = [jnp.concatenate([g[0], g[0], g[1], g[1]], axis=1) * dc for g, dc in zip(gram, decay)]
        y_intra = [_dot(_bf(sc), xb) for sc, xb in zip(scores, x_bd)]
        last = [cm[q - 1:q, :] if d == 0 else cm[0:1, :] for (d, _), cm in zip(tasks, cum)]
        upd = [_dot_tn(b, _bf(v * jnp.exp(la - cm))) for b, v, la, cm in zip(bm_bf, xdt, last, cum)]
        for t, (d, c) in enumerate(tasks):
            yi_s[d, rows[t], :] = y_intra[t]
            ec_s[d, rows[t], :] = jnp.exp(cum[t])
            upd_s[d, c] = jnp.where(state_mask, upd[t], 0.0)
            el_s[d, c] = jnp.exp(last[t])

    for j in range(nc):
        for d in range(2):
            c = j if d == 0 else nc - 1 - j
            r = slice(c * q, (c + 1) * q)
            s = s_ref[d]
            y_refs[d][r, :] = _dot(_bf(x_refs[d][r, 384:512]), _bf(s)) * ec_s[d, r, :] + yi_s[d, r, :]
            s_ref[d] = s * el_s[d, c] + upd_s[d, c]

    @pl.when(i == pl.num_programs(1) - 1)
    def _():
        hf_ref[...] = s_ref[...]


def _ssd_scan(lidx, xbc, small, dtb_row, alog_exp, alog_small, h0):
    bsz, seq, _ = xbc.shape
    tb = min(seq, 512)
    nb = seq // tb
    nc = tb // SSD_CHUNK
    fwd, bwd, par, state = _scan_specs(tb, nb)
    grid_spec = pltpu.PrefetchScalarGridSpec(
        num_scalar_prefetch=1,
        grid=(bsz, nb),
        in_specs=[fwd(512), fwd(SMALL_W), bwd(512), bwd(SMALL_W), par((1, SMALL_W)), par((2, 1, 256)),
                  par((1, SMALL_W)), state(128)],
        out_specs=[fwd(256), bwd(256), state(128)],
        scratch_shapes=[pltpu.VMEM((2, 128, 256), F32), pltpu.VMEM((2, tb, 256), F32), pltpu.VMEM((2, tb, 256), F32),
                        pltpu.VMEM((2, nc, 128, 256), F32), pltpu.VMEM((2, nc, 1, 256), F32)],
    )
    return pl.pallas_call(
        functools.partial(_ssd_kernel, nc=nc),
        out_shape=[jax.ShapeDtypeStruct((bsz, seq, 256), F32), jax.ShapeDtypeStruct((bsz, seq, 256), F32),
                   jax.ShapeDtypeStruct((bsz, 2, 128, 256), F32)],
        grid_spec=grid_spec,
        compiler_params=_cparams(),
        name="ssd_scan",
    )(lidx, xbc, small, xbc, small, dtb_row, alog_exp, alog_small, h0)


def _gdn_kernel(l_ref, qf_ref, kf_ref, vf_ref, smf_ref, qb_ref, kb_ref, vb_ref, smb_ref, dtb_ref, alog_ref, s0_ref,
                of_ref, ob_ref, sf_ref, s_ref, u_s, kt_s, wq_s, at_s, gl_s, *, nck):
    del l_ref
    cl = GDN_CHUNK
    i = pl.program_id(1)

    @pl.when(i == 0)
    def _():
        s_ref[...] = s0_ref[...]

    q_refs = (qf_ref, qb_ref)
    k_refs = (kf_ref, kb_ref)
    v_refs = (vf_ref, vb_ref)
    sm_refs = (smf_ref, smb_ref)
    o_refs = (of_ref, ob_ref)
    sign = (1, -1)
    ii = _iota((cl, D_GROUP), 0)
    jj = _iota((cl, D_GROUP), 1) & (cl - 1)
    low = [(ii - jj) * s >= 0 for s in sign]
    strict = [(ii - jj) * s > 0 for s in sign]
    diag = ii == jj
    blk16 = (ii >> 4) == (jj >> 4)
    blk32 = (ii >> 5) == (jj >> 5)
    off_masks = (blk32 & jnp.logical_not(blk16), jnp.logical_not(blk32))
    head_bd = (_iota((D_GROUP, D_GROUP), 0) >> 6) == (_iota((D_GROUP, D_GROUP), 1) >> 6)
    dif = _iota((cl, cl), 0) - _iota((cl, cl), 1)
    tri_bf = [jnp.where(dif * s >= 0, 1.0, 0.0).astype(BF16) for s in sign]
    src = _iota((SMALL_W, 2 * D_GROUP), 0)
    lane = _iota((SMALL_W, 2 * D_GROUP), 1)
    expand = [jnp.where(src == 8 + 8 * (lane >> 8) + 4 * d + ((lane >> 6) & 3), 1.0, 0.0).astype(BF16)
              for d in range(2)]
    a_row = [-jnp.exp(alog_ref[d]) for d in range(2)]

    def blockdiag(y_bf):
        return jnp.where(head_bd, jnp.concatenate([y_bf] * N_HEADS, axis=0), jnp.zeros((), BF16))

    def mul(xs, ys):
        return [_dot(_bf(x), blockdiag(_bf(y))) for x, y in zip(xs, ys)]

    g_exp, b_exp = [], []
    for d in range(2):
        sm = sm_refs[d][...]
        lane_s = _iota(sm.shape, 1)
        t = jnp.where((lane_s >= 8) & (lane_s < 16), _softplus(sm + dtb_ref[...]), _sigmoid(sm))
        e = _mm_xe(t, expand[d])
        g_exp.append(e[:, 0:D_GROUP] * a_row[d])
        b_exp.append(e[:, D_GROUP:2 * D_GROUP])

    def chunk_local(tasks):
        ds = [d for d, _ in tasks]
        rows = [slice(c * cl, (c + 1) * cl) for _, c in tasks]
        gc = [_mm_ex(tri_bf[d], g_exp[d][r]) for d, r in zip(ds, rows)]
        gc_row = [jnp.sum(jnp.where(diag, g, 0.0), axis=0, keepdims=True) for g in gc]
        decay = [jnp.exp(jnp.where(low[d], g - gr, NEG_BIG)) for d, g, gr in zip(ds, gc, gc_row)]
        yield
        kc = [k_refs[d][r, :] for d, r in zip(ds, rows)]
        qs = [q_refs[d][r, :] * (HEAD_DIM ** -0.5) for d, r in zip(ds, rows)]
        be = [b_exp[d][r] for d, r in zip(ds, rows)]
        kb = [k * b for k, b in zip(kc, be)]
        k_bd = [blockdiag(_bf(k)) for k in kc]
        kq = [_dot_nt(_bf(jnp.concatenate([x, y], axis=0)), kd) for x, y, kd in zip(kb, qs, k_bd)]
        a = [jnp.where(strict[d], x[0:cl] * dc, 0.0) for d, x, dc in zip(ds, kq, decay)]
        attn = [x[cl:2 * cl] * dc for x, dc in zip(kq, decay)]
        yield
        ad = [jnp.where(blk16, x, 0.0) for x in a]
        a2 = mul(ad, ad)
        yield
        a4 = mul(a2, a2)
        yield
        a8 = mul(a4, a4)
        yield
        m = [-x for x in ad]
        for p in (a2, a4, a8):
            m = [mm + pp + mp for mm, pp, mp in zip(m, p, mul(m, p))]
            yield
        for off in off_masks:
            o = [jnp.where(off, x, 0.0) for x in a]
            x = [oo + mo for oo, mo in zip(o, mul(m, o))]
            yield
            m = [mm - (xx + xm) for mm, xx, xm in zip(m, x, mul(x, m))]
            yield
        rv = [v_refs[d][r, :] * b for d, r, b in zip(ds, rows, be)]
        rw = [x * jnp.exp(g) for x, g in zip(kb, gc)]
        u = [x + mx for x, mx in zip(rv, mul(m, rv))]
        yield
        w = [x + mx for x, mx in zip(rw, mul(m, rw))]
        for t, (d, c) in enumerate(tasks):
            g_last = gc[t][cl - 1:cl, :] if d == 0 else gc[t][0:1, :]
            u_s[d, rows[t], :] = u[t]
            kt_s[d, rows[t], :] = kc[t] * jnp.exp(g_last - gc[t])
            wq_s[d, c, 0:cl, :] = w[t]
            wq_s[d, c, cl:2 * cl, :] = qs[t] * jnp.exp(gc[t])
            at_s[d, rows[t], :] = attn[t]
            gl_s[d, c] = jnp.exp(g_last)
        yield

    def recurrence(steps):
        for j in steps:
            cs = [j, nck - 1 - j]
            rows = [slice(c * cl, (c + 1) * cl) for c in cs]
            s = [s_ref[d] for d in range(2)]
            ws = [_dot(_bf(wq_s[d, cs[d]]), _bf(s[d])) for d in range(2)]
            yield
            v_bf = [_bf(u_s[d, rows[d], :] - ws[d][0:cl]) for d in range(2)]
            upd = [_dot_tn(_bf(kt_s[d, rows[d], :]), v_bf[d]) for d in range(2)]
            o = [_dot(_bf(at_s[d, rows[d], :]), blockdiag(v_bf[d])) + ws[d][cl:2 * cl] for d in range(2)]
            yield
            for d in range(2):
                o_refs[d][rows[d], :] = o[d]
                s_ref[d] = s[d] * gl_s[d, cs[d]] + jnp.where(head_bd, upd[d], 0.0)
            yield

    def interleave(*gens):
        live = list(gens)
        while live:
            live = [g for g in live if next(g, StopIteration) is not StopIteration]

    lead = nck // 2 if 2 * nck > SCAN_GROUP else nck
    first = [(0, c) for c in range(lead)] + [(1, c) for c in range(nck - 1, nck - 1 - lead, -1)]
    second = [(0, c) for c in range(lead, nck)] + [(1, c) for c in range(nck - 1 - lead, -1, -1)]
    interleave(chunk_local(first))
    if second:
        interleave(chunk_local(second), recurrence(range(lead)))
        interleave(recurrence(range(lead, nck)))
    else:
        interleave(recurrence(range(nck)))

    @pl.when(i == pl.num_programs(1) - 1)
    def _():
        sf_ref[...] = s_ref[...]


def _gdn_scan(lidx, q, k, v, small, dtb_row, alog_exp, s0):
    bsz, seq, _ = q.shape
    tb = min(seq, 512)
    nb = seq // tb
    nck = tb // GDN_CHUNK
    fwd, bwd, par, state = _scan_specs(tb, nb)
    grid_spec = pltpu.PrefetchScalarGridSpec(
        num_scalar_prefetch=1,
        grid=(bsz, nb),
        in_specs=[fwd(256), fwd(256), fwd(256), fwd(SMALL_W), bwd(256), bwd(256), bwd(256), bwd(SMALL_W),
                  par((1, SMALL_W)), par((2, 1, 256)), state(256)],
        out_specs=[fwd(256), bwd(256), state(256)],
        scratch_shapes=[pltpu.VMEM((2, 256, 256), F32), pltpu.VMEM((2, tb, 256), F32), pltpu.VMEM((2, tb, 256), F32),
                        pltpu.VMEM((2, nck, 2 * GDN_CHUNK, 256), F32), pltpu.VMEM((2, tb, 256), F32),
                        pltpu.VMEM((2, nck, 1, 256), F32)],
    )
    return pl.pallas_call(
        functools.partial(_gdn_kernel, nck=nck),
        out_shape=[jax.ShapeDtypeStruct((bsz, seq, 256), F32), jax.ShapeDtypeStruct((bsz, seq, 256), F32),
                   jax.ShapeDtypeStruct((bsz, 2, 256, 256), F32)],
        grid_spec=grid_spec,
        compiler_params=_cparams(),
        name="gdn_scan",
    )(lidx, q, k, v, small, q, k, v, small, dtb_row, alog_exp, s0)


FFN_STRIP = 256
EDGE = 2 * HALO_F32


def _tail_kernel(l_ref, x_ref, mod_ref, yf_ref, yb_ref, xs_ref, z_ref, of_ref, ob_ref, gate_ref, yp_ref, ym_ref,
                 dsk_ref, snw_ref, gnw_ref, wo_ref, l1w_ref, l1b_ref, wup_ref, cw_ref, cb_ref, wd_ref, l2w_ref, l2b_ref,
                 x2_ref, win_s, b_s, x1_s):
    del l_ref
    dm = D_MODEL
    halo = HALO_F32
    t = x_ref.shape[0]
    i = pl.program_id(1)
    n = pl.num_programs(1) - 1

    @pl.when(i == 0)
    def _():
        win_s[...] = jnp.zeros(win_s.shape, F32)
        b_s[...] = jnp.zeros(b_s.shape, F32)
        x1_s[...] = jnp.zeros(x1_s.shape, F32)

    mod = mod_ref[...]
    ys = (yf_ref[...] + yb_ref[...] + xs_ref[...] * dsk_ref[...]) * _silu(z_ref[...])
    ys = ys * lax.rsqrt(jnp.mean(ys * ys, axis=-1, keepdims=True) + RMS_EPS) * snw_ref[...]
    og = of_ref[...] + ob_ref[...]
    ms = _mm_xe(og * og, _head_ones()) * (1.0 / HEAD_DIM)
    og = og * lax.rsqrt(ms + RMS_EPS) * gnw_ref[...] * _silu(gate_ref[...])
    proj = (_dot(_bf(yp_ref[...]), wo_ref[256:512, :]) + _dot(_bf(ym_ref[...]), wo_ref[768:1024, :])
            + _dot(_bf(ys), wo_ref[0:256, :]) + _dot(_bf(og), wo_ref[512:768, :]))
    x1 = _ln(DN_ALPHA * x_ref[...] + mod[:, 2 * dm:3 * dm] * proj) * l1w_ref[...] + l1b_ref[...]
    h2 = _bf(_ln(x1) * (1.0 + mod[:, 4 * dm:5 * dm]) + mod[:, 3 * dm:4 * dm])

    def up(j):
        return (_dot(h2, wup_ref[:, j * FFN_STRIP:(j + 1) * FFN_STRIP]),
                _dot(h2, wup_ref[:, D_FF + j * FFN_STRIP:D_FF + (j + 1) * FFN_STRIP]))

    f = jnp.zeros((t, dm), F32)
    ab = up(0)
    for j in range(D_FF // FFN_STRIP):
        cols = slice(j * FFN_STRIP, (j + 1) * FFN_STRIP)
        ab_next = up(j + 1) if j + 1 < D_FF // FFN_STRIP else None
        a, b = ab
        w = cw_ref[:, cols]
        head = jnp.where(i < n, a[0:halo], 0.0)
        edge_win = jnp.concatenate([win_s[t - EDGE:t + halo, cols], head], axis=0)
        conv = jnp.concatenate([_conv_window(win_s[:, cols], w, FFN_CONV_W, halo, t - EDGE),
                                _conv_window(edge_win, w, FFN_CONV_W, halo, EDGE)], axis=0)
        f = f + _dot(_bf(_silu(conv + cb_ref[:, cols]) * b_s[:, cols]), wd_ref[cols, :])
        win_s[0:halo, cols] = win_s[t:t + halo, cols]
        win_s[halo:halo + t, cols] = a
        b_s[:, cols] = b
        ab = ab_next
    x2_ref[...] = _ln(DN_ALPHA * x1_s[...] + mod[:, 5 * dm:6 * dm] * f) * l2w_ref[...] + l2b_ref[...]
    x1_s[...] = x1


def _tail(lidx, x, mods, y_f, y_b, xbc, z, o_f, o_b, gate, y_pool, y_gmlp, p, is_ctx):
    bsz, seq, _ = x.shape
    t = min(seq, ROW_TILE)
    n = seq // t
    cur = lambda width: pl.BlockSpec((None, t, width), lambda b, i, l: (b, jnp.minimum(i, n - 1), 0))
    lag = lambda width: pl.BlockSpec((None, t, width), lambda b, i, l: (b, jnp.maximum(i - 1, 0), 0))
    grid_spec = pltpu.PrefetchScalarGridSpec(
        num_scalar_prefetch=1,
        grid=(bsz, n + 1),
        in_specs=[cur(D_MODEL), _mod_spec(is_ctx)] + [cur(256)] * 9
                 + [_layer_spec((1, 256)), _layer_spec((1, 256)), _layer_spec((1, 256)),
                    _layer_spec((D_MODEL, D_MODEL)), _layer_spec((1, D_MODEL)), _layer_spec((1, D_MODEL)),
                    _layer_spec((D_MODEL, 2 * D_FF)), _layer_spec((FFN_CONV_W, D_FF)), _layer_spec((1, D_FF)),
                    _layer_spec((D_FF, D_MODEL)), _layer_spec((1, D_MODEL)), _layer_spec((1, D_MODEL))],
        out_specs=lag(D_MODEL),
        scratch_shapes=[pltpu.VMEM((t + HALO_F32, D_FF), F32), pltpu.VMEM((t, D_FF), F32),
                        pltpu.VMEM((t, D_MODEL), F32)],
    )
    return pl.pallas_call(
        _tail_kernel,
        out_shape=jax.ShapeDtypeStruct((bsz, seq, D_MODEL), F32),
        grid_spec=grid_spec,
        compiler_params=_cparams(),
        name="tail",
    )(lidx, x, mods, y_f, y_b, xbc, z, o_f, o_b, gate, y_pool, y_gmlp, p["d_exp"], p["ssd_nw"], p["gdn_nw"],
      p["w_out"], p["ln1_w"], p["ln1_b"], p["ffn_up"], p["ffn_conv_w"], p["ffn_conv_b"], p["ffn_down"],
      p["ln2_w"], p["ln2_b"])


def _prep_params(w_in, ssd_conv_w, ssd_conv_b, ssd_dt_bias, ssd_a_log, ssd_d, ssd_norm_w, pool_w, pool_scale,
                 gdn_conv_w, gdn_dt_bias, gdn_a_log, gdn_norm_w, gmlp_ln_w, gmlp_ln_b, gmlp_ws, gmlp_bs, w_out,
                 ln1_w, ln1_b, ffn_up, ffn_conv_w, ffn_conv_b, ffn_down, ln2_w, ln2_b):
    nl = DEPTH
    cols = [w_in[:, :, 0:256], w_in[:, :, 256:768], w_in[:, :, 1032:1800], w_in[:, :, 776:1032],
            w_in[:, :, 1800:2056], w_in[:, :, 2072:2584], w_in[:, :, 768:776], w_in[:, :, 2056:2072],
            jnp.zeros((nl, D_MODEL, SMALL_W - 24), w_in.dtype)]
    row = lambda t: t.reshape(nl, 1, -1)
    pad_small = lambda t: jnp.concatenate([t, jnp.zeros((nl, SMALL_W - t.shape[-1]), F32)], axis=-1).reshape(nl, 1, SMALL_W)
    eye4 = jnp.eye(4, dtype=F32)
    pool_bd = (eye4[None, :, None, :, None] * pool_w[:, :, :, None, :]).reshape(nl, 256, 256)
    return dict(
        w_in=jnp.concatenate(cols, axis=-1).astype(BF16),
        conv_w=jnp.concatenate([ssd_conv_w, gdn_conv_w], axis=-1),
        conv_b=jnp.concatenate([ssd_conv_b, jnp.zeros((nl, 768), F32)], axis=-1).reshape(nl, 1, CONV_C),
        dtb_row=pad_small(jnp.concatenate([ssd_dt_bias.reshape(nl, 8), gdn_dt_bias.reshape(nl, 8)], axis=-1)),
        ssd_alog_exp=jnp.repeat(ssd_a_log, HEAD_DIM, axis=-1).reshape(nl, 2, 1, 256),
        ssd_alog_small=pad_small(ssd_a_log.reshape(nl, 8)),
        gdn_alog_exp=jnp.repeat(gdn_a_log, HEAD_DIM, axis=-1).reshape(nl, 2, 1, 256),
        d_exp=row(jnp.repeat(ssd_d, HEAD_DIM, axis=-1)),
        ssd_nw=row(ssd_norm_w),
        gdn_nw=row(jnp.tile(gdn_norm_w, (1, N_HEADS))),
        pool_wbd=pool_bd.astype(BF16),
        pool_scale=row(pool_scale),
        gmlp_ln_w=row(gmlp_ln_w),
        gmlp_ln_b=row(gmlp_ln_b),
        gmlp_ws=gmlp_ws.astype(BF16),
        gmlp_bs=jnp.repeat(jnp.swapaxes(gmlp_bs, 1, 2), HEAD_DIM, axis=-1),
        w_out=w_out.astype(BF16),
        ln1_w=row(ln1_w), ln1_b=row(ln1_b),
        ffn_up=ffn_up.astype(BF16),
        ffn_conv_w=ffn_conv_w,
        ffn_conv_b=row(ffn_conv_b),
        ffn_down=ffn_down.astype(BF16),
        ln2_w=row(ln2_w), ln2_b=row(ln2_b),
    )


def _stream_layer(lidx, x, mods, p, states0, is_ctx, full):
    z, gate, small, y_pool, y_gmlp, xbc, q, k, v = _front(lidx, x, mods, p, is_ctx)
    y_f, y_b, ssd_fin = _ssd_scan(lidx, xbc, small, p["dtb_row"], p["ssd_alog_exp"], p["ssd_alog_small"], states0[0])
    o_f, o_b, gdn_fin = _gdn_scan(lidx, q, k, v, small, p["dtb_row"], p["gdn_alog_exp"], states0[1])
    if not full:
        return None, (ssd_fin, gdn_fin)
    x2 = _tail(lidx, x, mods, y_f, y_b, xbc, z, o_f, o_b, gate, y_pool, y_gmlp, p, is_ctx)
    return x2, (ssd_fin, gdn_fin)


def kernel(x, c, ctx, c_ctx, w_mod, b_mod, w_in, ssd_conv_w, ssd_conv_b, ssd_dt_bias, ssd_a_log, ssd_d, ssd_norm_w, pool_w, pool_scale, gdn_conv_w, gdn_dt_bias, gdn_a_log, gdn_norm_w, gmlp_ln_w, gmlp_ln_b, gmlp_ws, gmlp_bs, w_out, ln1_w, ln1_b, ffn_up, ffn_conv_w, ffn_conv_b, ffn_down, ln2_w, ln2_b):
    bsz = x.shape[0]
    p = _prep_params(w_in, ssd_conv_w, ssd_conv_b, ssd_dt_bias, ssd_a_log, ssd_d, ssd_norm_w, pool_w, pool_scale,
                     gdn_conv_w, gdn_dt_bias, gdn_a_log, gdn_norm_w, gmlp_ln_w, gmlp_ln_b, gmlp_ws, gmlp_bs, w_out,
                     ln1_w, ln1_b, ffn_up, ffn_conv_w, ffn_conv_b, ffn_down, ln2_w, ln2_b)
    cs = jnp.concatenate([c_ctx[None, :], c, jnp.zeros((8 - 1 - bsz, D_MODEL), F32)], axis=0)
    mods = _modulation(cs, w_mod, b_mod).reshape(DEPTH, 8, 1, 6 * D_MODEL)
    zero_states = (jnp.zeros((bsz, 2, 128, 256), F32), jnp.zeros((bsz, 2, 256, 256), F32))
    lat, cx = x, ctx
    for l in range(DEPTH):
        lidx = jnp.full((1,), l, jnp.int32)
        cx_next, ctx_states = _stream_layer(lidx, cx, mods, p, zero_states, True, l < DEPTH - 1)
        lat, _ = _stream_layer(lidx, lat, mods, p, ctx_states, False, True)
        cx = cx_next
    return lat
```

```python
import functools

import jax
import jax.numpy as jnp
from jax import lax
from jax.experimental import pallas as pl
from jax.experimental.pallas import tpu as pltpu

F32 = jnp.float32
BF16 = jnp.bfloat16

D_MODEL = 1024
DEPTH = 4
D_GROUP = 256
N_HEADS = 4
HEAD_DIM = 64
SSD_CHUNK = 128
GDN_CHUNK = 64
GMLP_CHUNK = 128
GRID_W = 64
POOL_WINDOWS = (2, 4, 8, 16)
D_FF = 2816
CONV_W = 7
FFN_CONV_W = 3
CONV_C = 1280
N_PROJ = 2688
SMALL_W = 128
HALO_F32 = 8
LN_EPS = 1e-6
RMS_EPS = 1e-6
DN_ALPHA = (2 * DEPTH) ** 0.25
NEG_BIG = -1e30
VMEM_LIMIT = 56 * 1024 * 1024


def _iota(shape, dim):
    return lax.broadcasted_iota(jnp.int32, shape, dim)


def _bf(x):
    return x.astype(BF16)


def _dot(a, b):
    return jnp.dot(a, b, preferred_element_type=F32)


def _dot_nt(a, b):
    return lax.dot_general(a, b, (((1,), (1,)), ((), ())), preferred_element_type=F32)


def _dot_tn(a, b):
    return lax.dot_general(a, b, (((0,), (0,)), ((), ())), preferred_element_type=F32)


def _split3(x):
    x1 = _bf(x)
    r1 = x - x1.astype(F32)
    x2 = _bf(r1)
    x3 = _bf(r1 - x2.astype(F32))
    return x1, x2, x3


def _mm_xe(x, e):
    x1, x2, x3 = _split3(x)
    return _dot(x1, e) + _dot(x2, e) + _dot(x3, e)


def _mm_ex(e, x):
    x1, x2, x3 = _split3(x)
    return _dot(e, x1) + _dot(e, x2) + _dot(e, x3)


def _group_sum(x, ones_bd):
    x1 = _bf(x)
    x2 = _bf(x - x1.astype(F32))
    return _dot(x1, ones_bd) + _dot(x2, ones_bd)


def _mm3(a, b):
    ah = _bf(a)
    al = _bf(a - ah.astype(F32))
    bh = _bf(b)
    bl = _bf(b - bh.astype(F32))
    return _dot(ah, bh) + _dot(ah, bl) + _dot(al, bh)


def _sigmoid(x):
    return 0.5 * jnp.tanh(0.5 * x) + 0.5


def _silu(x):
    h = 0.5 * x
    return h + h * jnp.tanh(h)


def _softplus(x):
    return jnp.maximum(x, 0.0) + jnp.log1p(jnp.exp(-jnp.abs(x)))


def _gelu_tanh(x):
    return x * (0.5 * (1.0 + jnp.tanh(0.7978845608028654 * (x + 0.044715 * (x * x * x)))))


def _ln(x):
    mu = jnp.mean(x, axis=-1, keepdims=True)
    xc = x - mu
    var = jnp.mean(xc * xc, axis=-1, keepdims=True)
    return xc * lax.rsqrt(var + LN_EPS)


def _head_ones():
    lane = jnp.arange(D_GROUP) // HEAD_DIM
    return (lane[:, None] == lane[None, :]).astype(BF16)


def _cparams():
    return pltpu.CompilerParams(vmem_limit_bytes=VMEM_LIMIT)


def _resident(shape, index_map):
    return pl.BlockSpec(shape, index_map, pipeline_mode=pl.Buffered(1))


def _mod_kernel(c_ref, w_ref, b_ref, o_ref):
    o_ref[...] = _mm3(_silu(c_ref[...]), w_ref[...]) + b_ref[...]


def _modulation(cs, w_mod, b_mod):
    tn = 768
    n6 = 6 * D_MODEL
    return pl.pallas_call(
        _mod_kernel,
        out_shape=jax.ShapeDtypeStruct((DEPTH, 8, n6), F32),
        grid=(DEPTH, n6 // tn),
        in_specs=[pl.BlockSpec((8, D_MODEL), lambda l, j: (0, 0)),
                  pl.BlockSpec((None, D_MODEL, tn), lambda l, j: (l, 0, j)),
                  pl.BlockSpec((None, 1, tn), lambda l, j: (l, 0, j))],
        out_specs=pl.BlockSpec((None, 8, tn), lambda l, j: (l, 0, j)),
        compiler_params=_cparams(),
        name="modulation",
    )(cs, w_mod, b_mod.reshape(DEPTH, 1, n6))


ROW_TILE = 256


def _mod_spec(is_ctx):
    if is_ctx:
        return pl.BlockSpec((None, None, 1, 6 * D_MODEL), lambda b, i, l: (l[0], 0, 0, 0))
    return pl.BlockSpec((None, None, 1, 6 * D_MODEL), lambda b, i, l: (l[0], b + 1, 0, 0))


def _layer_spec(shape):
    nd = len(shape)
    return _resident((None,) + tuple(shape), lambda b, i, l: (l[0],) + (0,) * nd)


def _conv_window(win, w, width, halo, t):
    n = win.shape[0]
    half = width // 2
    acc = None
    for k in range(width):
        shift = (half - k) % n
        r = pltpu.roll(win, shift, 0) if shift else win
        term = r[halo:halo + t] * w[k:k + 1, :]
        acc = term if acc is None else acc + term
    return acc


def _pool_gmlp(x, uv, pw_ref, ps_ref, lw_ref, lb_ref, ws_ref, bs_ref, yp_ref, ym_ref, row_w):
    t = x.shape[0]
    lane_grp = _iota((t, D_GROUP), 1) >> 6
    ii = _iota((t, t), 0)
    jj = _iota((t, t), 1)
    base = ii - (ii & (row_w - 1))
    pos = ii & (row_w - 1)
    pos_c = _iota((t, D_GROUP), 0) & (row_w - 1)
    pooled = jnp.zeros((t, D_GROUP), F32)
    x_terms = _split3(x)
    for g, w in enumerate(POOL_WINDOWS):
        lo = jnp.maximum(pos - w // 2, 0)
        hi = jnp.minimum(pos + (w - w // 2), row_w)
        band = jnp.where((jj >= base + lo) & (jj < base + hi), 1.0, 0.0).astype(BF16)
        cnt = (jnp.minimum(pos_c + (w - w // 2), row_w) - jnp.maximum(pos_c - w // 2, 0)).astype(F32)
        in_grp = lane_grp == g
        window_sum = sum(_dot(band, jnp.where(in_grp, xt, jnp.zeros((), BF16))) for xt in x_terms)
        pooled = pooled + window_sum / cnt
    yp_ref[...] = _dot(_bf(pooled - x), pw_ref[...]) * ps_ref[...]

    gl = _gelu_tanh(uv)
    u = gl[:, 0:D_GROUP]
    v = _ln(gl[:, D_GROUP:2 * D_GROUP]) * lw_ref[...] + lb_ref[...]
    grp = _iota((GMLP_CHUNK, D_GROUP), 1) >> 6
    for c in range(t // GMLP_CHUNK):
        rows = slice(c * GMLP_CHUNK, (c + 1) * GMLP_CHUNK)
        vc = v[rows]
        acc = bs_ref[...]
        for g in range(4):
            acc = acc + _dot(ws_ref[g], _bf(jnp.where(grp == g, vc, 0.0)))
        ym_ref[rows, :] = u[rows] * acc


def _front_kernel(l_ref, x_ref, mod_ref, w_ref, cw_ref, cb_ref, ones_ref, pw_ref, ps_ref, lw_ref, lb_ref, ws_ref, bs_ref,
                  z_ref, gate_ref, sm_ref, yp_ref, ym_ref, xbc_ref, q_ref, k_ref, v_ref, win_s, *, row_w, lagged):
    del l_ref
    t = x_ref.shape[0]
    halo = HALO_F32
    i = pl.program_id(1)
    n = pl.num_programs(1) - 1

    if lagged:
        @pl.when(i == 0)
        def _():
            win_s[...] = jnp.zeros(win_s.shape, F32)

    mod = mod_ref[...]
    h = _ln(x_ref[...]) * (1.0 + mod[:, D_MODEL:2 * D_MODEL]) + mod[:, 0:D_MODEL]
    p = _dot(_bf(h), w_ref[...])
    z_ref[...] = p[:, 0:256]
    gate_ref[...] = p[:, 1792:2048]
    sm_ref[...] = p[:, 2560:2688]
    _pool_gmlp(p[:, 1536:1792], p[:, 2048:2560], pw_ref, ps_ref, lw_ref, lb_ref, ws_ref, bs_ref, yp_ref, ym_ref, row_w)

    cv = p[:, 256:256 + CONV_C]
    if lagged:
        win_s[halo + t:2 * halo + t, :] = jnp.where(i < n, cv[0:halo], 0.0)
    ones_bd = ones_ref[...]
    pad = jnp.zeros((halo, D_GROUP), F32)
    outs = (None, None, q_ref, k_ref, v_ref)
    for s in range(CONV_C // D_GROUP):
        sl = slice(s * D_GROUP, (s + 1) * D_GROUP)
        win = win_s[:, sl] if lagged else jnp.concatenate([pad, cv[:, sl], pad], axis=0)
        y = _silu(_conv_window(win, cw_ref[:, sl], CONV_W, halo, t) + cb_ref[:, sl])
        if s < 2:
            xbc_ref[:, sl] = y
        elif s < 4:
            outs[s][...] = y * lax.rsqrt(_group_sum(y * y, ones_bd) + 1e-6)
        else:
            outs[s][...] = y
    if lagged:
        win_s[0:halo, :] = win_s[t:t + halo, :]
        win_s[halo:halo + t, :] = cv


def _tile_specs(t, n):
    if n == 1:
        same = lambda width: pl.BlockSpec((None, t, width), lambda b, i, l: (b, 0, 0))
        return same, same, 1
    cur = lambda width: pl.BlockSpec((None, t, width), lambda b, i, l: (b, jnp.minimum(i, n - 1), 0))
    lag = lambda width: pl.BlockSpec((None, t, width), lambda b, i, l: (b, jnp.maximum(i - 1, 0), 0))
    return cur, lag, n + 1


def _const_spec(shape):
    return _resident(tuple(shape), lambda b, i, l: (0,) * len(shape))


def _front(lidx, x, mods, p, is_ctx):
    bsz, seq, _ = x.shape
    t = min(seq, ROW_TILE)
    n = seq // t
    row_w = seq if is_ctx else GRID_W
    assert not is_ctx or t == seq
    cur, lag, steps = _tile_specs(t, n)
    grid_spec = pltpu.PrefetchScalarGridSpec(
        num_scalar_prefetch=1,
        grid=(bsz, steps),
        in_specs=[cur(D_MODEL), _mod_spec(is_ctx), _layer_spec((D_MODEL, N_PROJ)), _layer_spec((CONV_W, CONV_C)),
                  _layer_spec((1, CONV_C)), _const_spec((256, 256)),
                  _layer_spec((256, 256)), _layer_spec((1, 256)), _layer_spec((1, 256)),
                  _layer_spec((1, 256)), _layer_spec((4, GMLP_CHUNK, GMLP_CHUNK)), _layer_spec((GMLP_CHUNK, 256))],
        out_specs=[cur(256), cur(256), cur(SMALL_W), cur(256), cur(256), lag(512), lag(256), lag(256), lag(256)],
        scratch_shapes=[pltpu.VMEM((t + 2 * HALO_F32, CONV_C), F32)],
    )
    widths = (256, 256, SMALL_W, 256, 256, 512, 256, 256, 256)
    return pl.pallas_call(
        functools.partial(_front_kernel, row_w=row_w, lagged=n > 1),
        out_shape=[jax.ShapeDtypeStruct((bsz, seq, w), F32) for w in widths],
        grid_spec=grid_spec,
        compiler_params=_cparams(),
        name="front",
    )(lidx, x, mods, p["w_in"], p["conv_w"], p["conv_b"], _head_ones(), p["pool_wbd"], p["pool_scale"],
      p["gmlp_ln_w"], p["gmlp_ln_b"], p["gmlp_ws"], p["gmlp_bs"])


SCAN_GROUP = 8


def _scan_specs(tb, nb):
    fwd = lambda width: pl.BlockSpec((None, tb, width), lambda b, i, l: (b, i, 0))
    bwd = lambda width: pl.BlockSpec((None, tb, width), lambda b, i, l: (b, nb - 1 - i, 0))
    par = lambda shape: _resident((None,) + tuple(shape), lambda b, i, l: (l[0],) + (0,) * len(shape))
    state = lambda rows: pl.BlockSpec((None, 2, rows, 256), lambda b, i, l: (b, 0, 0, 0))
    return fwd, bwd, par, state


def _groups(tasks):
    return [tasks[s:s + SCAN_GROUP] for s in range(0, len(tasks), SCAN_GROUP)]


def _ssd_kernel(l_ref, xf_ref, smf_ref, xb_ref, smb_ref, dtb_ref, alog_ref, asm_ref, h0_ref, yf_ref, yb_ref, hf_ref,
                s_ref, yi_s, ec_s, upd_s, el_s, *, nc):
    del l_ref
    q = SSD_CHUNK
    i = pl.program_id(1)

    @pl.when(i == 0)
    def _():
        s_ref[...] = h0_ref[...]

    x_refs = (xf_ref, xb_ref)
    sm_refs = (smf_ref, smb_ref)
    y_refs = (yf_ref, yb_ref)
    sign = (1, -1)
    dif = _iota((q, q), 0) - _iota((q, q), 1)
    dif4 = _iota((q, 4 * q), 0) - (_iota((q, 4 * q), 1) & (q - 1))
    tri_bf = [jnp.where(dif * s >= 0, 1.0, 0.0).astype(BF16) for s in sign]
    tri4 = [dif4 * s >= 0 for s in sign]
    src = _iota((SMALL_W, D_GROUP), 0)
    expand = [jnp.where(src == 4 * d + (_iota((SMALL_W, D_GROUP), 1) >> 6), 1.0, 0.0).astype(BF16) for d in range(2)]
    src4 = _iota((SMALL_W, 4 * q), 0)
    expand4 = [jnp.where(src4 == 4 * d + (_iota((SMALL_W, 4 * q), 1) >> 7), 1.0, 0.0).astype(BF16) for d in range(2)]
    a_row = [-jnp.exp(alog_ref[d]) for d in range(2)]
    a_small = -jnp.exp(asm_ref[...])
    lane_c = _iota((q, q), 1) >> 6
    head_rows = (_iota((4 * q, D_GROUP), 0) >> 7) == (_iota((4 * q, D_GROUP), 1) >> 6)
    state_mask = (_iota((q, D_GROUP), 0) >> 6) == (_iota((q, D_GROUP), 1) >> 7)

    dt_exp, dta, dts = [], [], []
    for d in range(2):
        dt_all = _softplus(sm_refs[d][...] + dtb_ref[...])
        e = _mm_xe(dt_all, expand[d])
        dt_exp.append(e)
        dta.append(e * a_row[d])
        dts.append(dt_all * a_small)

    for tasks in _groups([(d, c) for c in range(nc) for d in range(2)]):
        rows = [slice(c * q, (c + 1) * q) for _, c in tasks]
        cum = [_mm_ex(tri_bf[d], dta[d][r]) for (d, _), r in zip(tasks, rows)]
        cums = [_mm_ex(tri_bf[d], dts[d][r]) for (d, _), r in zip(tasks, rows)]
        col4 = [_mm_xe(cs, expand4[d]) for (d, _), cs in zip(tasks, cums)]
        cum_t = [cs.T for cs in cums]
        row4 = [jnp.concatenate([ct[4 * d + h:4 * d + h + 1, :] for h in range(N_HEADS)], axis=1)
                for (d, _), ct in zip(tasks, cum_t)]
        decay = [jnp.exp(jnp.where(tri4[d], c4 - r4, NEG_BIG)) for (d, _), c4, r4 in zip(tasks, col4, row4)]
        xc = [x_refs[d][r, :] for (d, _), r in zip(tasks, rows)]
        bm_bf = [_bf(x[:, 256:384]) for x in xc]
        gram = [[_dot_nt(_bf(jnp.where(lane_c == g, x[:, 384:512], 0.0)), b) for g in range(2)]
                for x, b in zip(xc, bm_bf)]
        xdt = [x[:, 0:256] * dt_exp[d][r] for (d, _), x, r in zip(tasks, xc, rows)]
        x_bd = [jnp.where(head_rows, jnp.concatenate([_bf(v)] * N_HEADS, axis=0), jnp.zeros((), BF16)) for v in xdt]
        scores = [jnp.concatenate([g[0], g[0], g[1], g[1]], axis=1) * dc for g, dc in zip(gram, decay)]
        y_intra = [_dot(_bf(sc), xb) for sc, xb in zip(scores, x_bd)]
        last = [cm[q - 1:q, :] if d == 0 else cm[0:1, :] for (d, _), cm in zip(tasks, cum)]
        upd = [_dot_tn(b, _bf(v * jnp.exp(la - cm))) for b, v, la, cm in zip(bm_bf, xdt, last, cum)]
        for t, (d, c) in enumerate(tasks):
            yi_s[d, rows[t], :] = y_intra[t]
            ec_s[d, rows[t], :] = jnp.exp(cum[t])
            upd_s[d, c] = jnp.where(state_mask, upd[t], 0.0)
            el_s[d, c] = jnp.exp(last[t])

    for j in range(nc):
        for d in range(2):
            c = j if d == 0 else nc - 1 - j
            r = slice(c * q, (c + 1) * q)
            s = s_ref[d]
            y_refs[d][r, :] = _dot(_bf(x_refs[d][r, 384:512]), _bf(s)) * ec_s[d, r, :] + yi_s[d, r, :]
            s_ref[d] = s * el_s[d, c] + upd_s[d, c]

    @pl.when(i == pl.num_programs(1) - 1)
    def _():
        hf_ref[...] = s_ref[...]


def _ssd_scan(lidx, xbc, small, dtb_row, alog_exp, alog_small, h0):
    bsz, seq, _ = xbc.shape
    tb = min(seq, 512)
    nb = seq // tb
    nc = tb // SSD_CHUNK
    fwd, bwd, par, state = _scan_specs(tb, nb)
    grid_spec = pltpu.PrefetchScalarGridSpec(
        num_scalar_prefetch=1,
        grid=(bsz, nb),
        in_specs=[fwd(512), fwd(SMALL_W), bwd(512), bwd(SMALL_W), par((1, SMALL_W)), par((2, 1, 256)),
                  par((1, SMALL_W)), state(128)],
        out_specs=[fwd(256), bwd(256), state(128)],
        scratch_shapes=[pltpu.VMEM((2, 128, 256), F32), pltpu.VMEM((2, tb, 256), F32), pltpu.VMEM((2, tb, 256), F32),
                        pltpu.VMEM((2, nc, 128, 256), F32), pltpu.VMEM((2, nc, 1, 256), F32)],
    )
    return pl.pallas_call(
        functools.partial(_ssd_kernel, nc=nc),
        out_shape=[jax.ShapeDtypeStruct((bsz, seq, 256), F32), jax.ShapeDtypeStruct((bsz, seq, 256), F32),
                   jax.ShapeDtypeStruct((bsz, 2, 128, 256), F32)],
        grid_spec=grid_spec,
        compiler_params=_cparams(),
        name="ssd_scan",
    )(lidx, xbc, small, xbc, small, dtb_row, alog_exp, alog_small, h0)


def _gdn_kernel(l_ref, qf_ref, kf_ref, vf_ref, smf_ref, qb_ref, kb_ref, vb_ref, smb_ref, dtb_ref, alog_ref, s0_ref,
                of_ref, ob_ref, sf_ref, s_ref, u_s, kt_s, wq_s, at_s, gl_s, *, nck):
    del l_ref
    cl = GDN_CHUNK
    i = pl.program_id(1)

    @pl.when(i == 0)
    def _():
        s_ref[...] = s0_ref[...]

    q_refs = (qf_ref, qb_ref)
    k_refs = (kf_ref, kb_ref)
    v_refs = (vf_ref, vb_ref)
    sm_refs = (smf_ref, smb_ref)
    o_refs = (of_ref, ob_ref)
    sign = (1, -1)
    ii = _iota((cl, D_GROUP), 0)
    jj = _iota((cl, D_GROUP), 1) & (cl - 1)
    low = [(ii - jj) * s >= 0 for s in sign]
    strict = [(ii - jj) * s > 0 for s in sign]
    diag = ii == jj
    blk16 = (ii >> 4) == (jj >> 4)
    blk32 = (ii >> 5) == (jj >> 5)
    off_masks = (blk32 & jnp.logical_not(blk16), jnp.logical_not(blk32))
    head_bd = (_iota((D_GROUP, D_GROUP), 0) >> 6) == (_iota((D_GROUP, D_GROUP), 1) >> 6)
    dif = _iota((cl, cl), 0) - _iota((cl, cl), 1)
    tri_bf = [jnp.where(dif * s >= 0, 1.0, 0.0).astype(BF16) for s in sign]
    src = _iota((SMALL_W, 2 * D_GROUP), 0)
    lane = _iota((SMALL_W, 2 * D_GROUP), 1)
    expand = [jnp.where(src == 8 + 8 * (lane >> 8) + 4 * d + ((lane >> 6) & 3), 1.0, 0.0).astype(BF16)
              for d in range(2)]
    a_row = [-jnp.exp(alog_ref[d]) for d in range(2)]

    def blockdiag(y_bf):
        return jnp.where(head_bd, jnp.concatenate([y_bf] * N_HEADS, axis=0), jnp.zeros((), BF16))

    def mul(xs, ys):
        return [_dot(_bf(x), blockdiag(_bf(y))) for x, y in zip(xs, ys)]

    g_exp, b_exp = [], []
    for d in range(2):
        sm = sm_refs[d][...]
        lane_s = _iota(sm.shape, 1)
        t = jnp.where((lane_s >= 8) & (lane_s < 16), _softplus(sm + dtb_ref[...]), _sigmoid(sm))
        e = _mm_xe(t, expand[d])
        g_exp.append(e[:, 0:D_GROUP] * a_row[d])
        b_exp.append(e[:, D_GROUP:2 * D_GROUP])

    def chunk_local(tasks):
        ds = [d for d, _ in tasks]
        rows = [slice(c * cl, (c + 1) * cl) for _, c in tasks]
        gc = [_mm_ex(tri_bf[d], g_exp[d][r]) for d, r in zip(ds, rows)]
        gc_row = [jnp.sum(jnp.where(diag, g, 0.0), axis=0, keepdims=True) for g in gc]
        decay = [jnp.exp(jnp.where(low[d], g - gr, NEG_BIG)) for d, g, gr in zip(ds, gc, gc_row)]
        yield
        kc = [k_refs[d][r, :] for d, r in zip(ds, rows)]
        qs = [q_refs[d][r, :] * (HEAD_DIM ** -0.5) for d, r in zip(ds, rows)]
        be = [b_exp[d][r] for d, r in zip(ds, rows)]
        kb = [k * b for k, b in zip(kc, be)]
        k_bd = [blockdiag(_bf(k)) for k in kc]
        kq = [_dot_nt(_bf(jnp.concatenate([x, y], axis=0)), kd) for x, y, kd in zip(kb, qs, k_bd)]
        a = [jnp.where(strict[d], x[0:cl] * dc, 0.0) for d, x, dc in zip(ds, kq, decay)]
        attn = [x[cl:2 * cl] * dc for x, dc in zip(kq, decay)]
        yield
        ad = [jnp.where(blk16, x, 0.0) for x in a]
        a2 = mul(ad, ad)
        yield
        a4 = mul(a2, a2)
        yield
        a8 = mul(a4, a4)
        yield
        m = [-x for x in ad]
        for p in (a2, a4, a8):
            m = [mm + pp + mp for mm, pp, mp in zip(m, p, mul(m, p))]
            yield
        for off in off_masks:
            o = [jnp.where(off, x, 0.0) for x in a]
            x = [oo + mo for oo, mo in zip(o, mul(m, o))]
            yield
            m = [mm - (xx + xm) for mm, xx, xm in zip(m, x, mul(x, m))]
            yield
        rv = [v_refs[d][r, :] * b for d, r, b in zip(ds, rows, be)]
        rw = [x * jnp.exp(g) for x, g in zip(kb, gc)]
        u = [x + mx for x, mx in zip(rv, mul(m, rv))]
        yield
        w = [x + mx for x, mx in zip(rw, mul(m, rw))]
        for t, (d, c) in enumerate(tasks):
            g_last = gc[t][cl - 1:cl, :] if d == 0 else gc[t][0:1, :]
            u_s[d, rows[t], :] = u[t]
            kt_s[d, rows[t], :] = kc[t] * jnp.exp(g_last - gc[t])
            wq_s[d, c, 0:cl, :] = w[t]
            wq_s[d, c, cl:2 * cl, :] = qs[t] * jnp.exp(gc[t])
            at_s[d, rows[t], :] = attn[t]
            gl_s[d, c] = jnp.exp(g_last)
        yield

    def recurrence(steps):
        for j in steps:
            cs = [j, nck - 1 - j]
            rows = [slice(c * cl, (c + 1) * cl) for c in cs]
            s = [s_ref[d] for d in range(2)]
            ws = [_dot(_bf(wq_s[d, cs[d]]), _bf(s[d])) for d in range(2)]
            yield
            v_bf = [_bf(u_s[d, rows[d], :] - ws[d][0:cl]) for d in range(2)]
            upd = [_dot_tn(_bf(kt_s[d, rows[d], :]), v_bf[d]) for d in range(2)]
            o = [_dot(_bf(at_s[d, rows[d], :]), blockdiag(v_bf[d])) + ws[d][cl:2 * cl] for d in range(2)]
            yield
            for d in range(2):
                o_refs[d][rows[d], :] = o[d]
                s_ref[d] = s[d] * gl_s[d, cs[d]] + jnp.where(head_bd, upd[d], 0.0)
            yield

    def interleave(*gens):
        live = list(gens)
        while live:
            live = [g for g in live if next(g, StopIteration) is not StopIteration]

    lead = nck // 2 if 2 * nck > SCAN_GROUP else nck
    first = [(0, c) for c in range(lead)] + [(1, c) for c in range(nck - 1, nck - 1 - lead, -1)]
    second = [(0, c) for c in range(lead, nck)] + [(1, c) for c in range(nck - 1 - lead, -1, -1)]
    interleave(chunk_local(first))
    if second:
        interleave(chunk_local(second), recurrence(range(lead)))
        interleave(recurrence(range(lead, nck)))
    else:
        interleave(recurrence(range(nck)))

    @pl.when(i == pl.num_programs(1) - 1)
    def _():
        sf_ref[...] = s_ref[...]


def _gdn_scan(lidx, q, k, v, small, dtb_row, alog_exp, s0):
    bsz, seq, _ = q.shape
    tb = min(seq, 512)
    nb = seq // tb
    nck = tb // GDN_CHUNK
    fwd, bwd, par, state = _scan_specs(tb, nb)
    grid_spec = pltpu.PrefetchScalarGridSpec(
        num_scalar_prefetch=1,
        grid=(bsz, nb),
        in_specs=[fwd(256), fwd(256), fwd(256), fwd(SMALL_W), bwd(256), bwd(256), bwd(256), bwd(SMALL_W),
                  par((1, SMALL_W)), par((2, 1, 256)), state(256)],
        out_specs=[fwd(256), bwd(256), state(256)],
        scratch_shapes=[pltpu.VMEM((2, 256, 256), F32), pltpu.VMEM((2, tb, 256), F32), pltpu.VMEM((2, tb, 256), F32),
                        pltpu.VMEM((2, nck, 2 * GDN_CHUNK, 256), F32), pltpu.VMEM((2, tb, 256), F32),
                        pltpu.VMEM((2, nck, 1, 256), F32)],
    )
    return pl.pallas_call(
        functools.partial(_gdn_kernel, nck=nck),
        out_shape=[jax.ShapeDtypeStruct((bsz, seq, 256), F32), jax.ShapeDtypeStruct((bsz, seq, 256), F32),
                   jax.ShapeDtypeStruct((bsz, 2, 256, 256), F32)],
        grid_spec=grid_spec,
        compiler_params=_cparams(),
        name="gdn_scan",
    )(lidx, q, k, v, small, q, k, v, small, dtb_row, alog_exp, s0)


FFN_STRIP = 256
EDGE = 2 * HALO_F32


def _tail_kernel(l_ref, x_ref, mod_ref, yf_ref, yb_ref, xs_ref, z_ref, of_ref, ob_ref, gate_ref, yp_ref, ym_ref,
                 dsk_ref, snw_ref, gnw_ref, ones_ref, wo_ref, l1w_ref, l1b_ref, wup_ref, cw_ref, cb_ref, wd_ref, l2w_ref,
                 l2b_ref, x2_ref, win_s, b_s, x1_s, *, lagged):
    del l_ref
    dm = D_MODEL
    halo = HALO_F32
    t = x_ref.shape[0]
    i = pl.program_id(1)
    n = pl.num_programs(1) - 1

    if lagged:
        @pl.when(i == 0)
        def _():
            win_s[...] = jnp.zeros(win_s.shape, F32)
            b_s[...] = jnp.zeros(b_s.shape, F32)
            x1_s[...] = jnp.zeros(x1_s.shape, F32)

    mod = mod_ref[...]
    ys = (yf_ref[...] + yb_ref[...] + xs_ref[...] * dsk_ref[...]) * _silu(z_ref[...])
    ys = ys * lax.rsqrt(jnp.mean(ys * ys, axis=-1, keepdims=True) + RMS_EPS) * snw_ref[...]
    og = of_ref[...] + ob_ref[...]
    ms = _group_sum(og * og, ones_ref[...]) * (1.0 / HEAD_DIM)
    og = og * lax.rsqrt(ms + RMS_EPS) * gnw_ref[...] * _silu(gate_ref[...])
    proj = (_dot(_bf(yp_ref[...]), wo_ref[256:512, :]) + _dot(_bf(ym_ref[...]), wo_ref[768:1024, :])
            + _dot(_bf(ys), wo_ref[0:256, :]) + _dot(_bf(og), wo_ref[512:768, :]))
    x1 = _ln(DN_ALPHA * x_ref[...] + mod[:, 2 * dm:3 * dm] * proj) * l1w_ref[...] + l1b_ref[...]
    h2 = _bf(_ln(x1) * (1.0 + mod[:, 4 * dm:5 * dm]) + mod[:, 3 * dm:4 * dm])

    def up(j):
        return (_dot(h2, wup_ref[:, j * FFN_STRIP:(j + 1) * FFN_STRIP]),
                _dot(h2, wup_ref[:, D_FF + j * FFN_STRIP:D_FF + (j + 1) * FFN_STRIP]))

    f = jnp.zeros((t, dm), F32)
    pad = jnp.zeros((halo, FFN_STRIP), F32)
    ab = up(0)
    for j in range(D_FF // FFN_STRIP):
        cols = slice(j * FFN_STRIP, (j + 1) * FFN_STRIP)
        ab_next = up(j + 1) if j + 1 < D_FF // FFN_STRIP else None
        a, b = ab
        w = cw_ref[:, cols]
        if lagged:
            head = jnp.where(i < n, a[0:halo], 0.0)
            edge_win = jnp.concatenate([win_s[t - EDGE:t + halo, cols], head], axis=0)
            conv = jnp.concatenate([_conv_window(win_s[:, cols], w, FFN_CONV_W, halo, t - EDGE),
                                    _conv_window(edge_win, w, FFN_CONV_W, halo, EDGE)], axis=0)
            gate = b_s[:, cols]
        else:
            conv = _conv_window(jnp.concatenate([pad, a, pad], axis=0), w, FFN_CONV_W, halo, t)
            gate = b
        f = f + _dot(_bf(_silu(conv + cb_ref[:, cols]) * gate), wd_ref[cols, :])
        if lagged:
            win_s[0:halo, cols] = win_s[t:t + halo, cols]
            win_s[halo:halo + t, cols] = a
            b_s[:, cols] = b
        ab = ab_next
    x1_prev = x1_s[...] if lagged else x1
    x2_ref[...] = _ln(DN_ALPHA * x1_prev + mod[:, 5 * dm:6 * dm] * f) * l2w_ref[...] + l2b_ref[...]
    if lagged:
        x1_s[...] = x1


def _tail(lidx, x, mods, y_f, y_b, xbc, z, o_f, o_b, gate, y_pool, y_gmlp, p, is_ctx):
    bsz, seq, _ = x.shape
    t = min(seq, ROW_TILE)
    n = seq // t
    cur, lag, steps = _tile_specs(t, n)
    grid_spec = pltpu.PrefetchScalarGridSpec(
        num_scalar_prefetch=1,
        grid=(bsz, steps),
        in_specs=[cur(D_MODEL), _mod_spec(is_ctx)] + [cur(256)] * 9
                 + [_layer_spec((1, 256)), _layer_spec((1, 256)), _layer_spec((1, 256)), _const_spec((256, 256)),
                    _layer_spec((D_MODEL, D_MODEL)), _layer_spec((1, D_MODEL)), _layer_spec((1, D_MODEL)),
                    _layer_spec((D_MODEL, 2 * D_FF)), _layer_spec((FFN_CONV_W, D_FF)), _layer_spec((1, D_FF)),
                    _layer_spec((D_FF, D_MODEL)), _layer_spec((1, D_MODEL)), _layer_spec((1, D_MODEL))],
        out_specs=lag(D_MODEL),
        scratch_shapes=[pltpu.VMEM((t + HALO_F32, D_FF), F32), pltpu.VMEM((t, D_FF), F32),
                        pltpu.VMEM((t, D_MODEL), F32)],
    )
    return pl.pallas_call(
        functools.partial(_tail_kernel, lagged=n > 1),
        out_shape=jax.ShapeDtypeStruct((bsz, seq, D_MODEL), F32),
        grid_spec=grid_spec,
        compiler_params=_cparams(),
        name="tail",
    )(lidx, x, mods, y_f, y_b, xbc, z, o_f, o_b, gate, y_pool, y_gmlp, p["d_exp"], p["ssd_nw"], p["gdn_nw"],
      _head_ones(), p["w_out"], p["ln1_w"], p["ln1_b"], p["ffn_up"], p["ffn_conv_w"], p["ffn_conv_b"], p["ffn_down"],
      p["ln2_w"], p["ln2_b"])


def _prep_params(w_in, ssd_conv_w, ssd_conv_b, ssd_dt_bias, ssd_a_log, ssd_d, ssd_norm_w, pool_w, pool_scale,
                 gdn_conv_w, gdn_dt_bias, gdn_a_log, gdn_norm_w, gmlp_ln_w, gmlp_ln_b, gmlp_ws, gmlp_bs, w_out,
                 ln1_w, ln1_b, ffn_up, ffn_conv_w, ffn_conv_b, ffn_down, ln2_w, ln2_b):
    nl = DEPTH
    cols = [w_in[:, :, 0:256], w_in[:, :, 256:768], w_in[:, :, 1032:1800], w_in[:, :, 776:1032],
            w_in[:, :, 1800:2056], w_in[:, :, 2072:2584], w_in[:, :, 768:776], w_in[:, :, 2056:2072],
            jnp.zeros((nl, D_MODEL, SMALL_W - 24), w_in.dtype)]
    row = lambda t: t.reshape(nl, 1, -1)
    pad_small = lambda t: jnp.concatenate([t, jnp.zeros((nl, SMALL_W - t.shape[-1]), F32)], axis=-1).reshape(nl, 1, SMALL_W)
    eye4 = jnp.eye(4, dtype=F32)
    pool_bd = (eye4[None, :, None, :, None] * pool_w[:, :, :, None, :]).reshape(nl, 256, 256)
    return dict(
        w_in=jnp.concatenate(cols, axis=-1).astype(BF16),
        conv_w=jnp.concatenate([ssd_conv_w, gdn_conv_w], axis=-1),
        conv_b=jnp.concatenate([ssd_conv_b, jnp.zeros((nl, 768), F32)], axis=-1).reshape(nl, 1, CONV_C),
        dtb_row=pad_small(jnp.concatenate([ssd_dt_bias.reshape(nl, 8), gdn_dt_bias.reshape(nl, 8)], axis=-1)),
        ssd_alog_exp=jnp.repeat(ssd_a_log, HEAD_DIM, axis=-1).reshape(nl, 2, 1, 256),
        ssd_alog_small=pad_small(ssd_a_log.reshape(nl, 8)),
        gdn_alog_exp=jnp.repeat(gdn_a_log, HEAD_DIM, axis=-1).reshape(nl, 2, 1, 256),
        d_exp=row(jnp.repeat(ssd_d, HEAD_DIM, axis=-1)),
        ssd_nw=row(ssd_norm_w),
        gdn_nw=row(jnp.tile(gdn_norm_w, (1, N_HEADS))),
        pool_wbd=pool_bd.astype(BF16),
        pool_scale=row(pool_scale),
        gmlp_ln_w=row(gmlp_ln_w),
        gmlp_ln_b=row(gmlp_ln_b),
        gmlp_ws=gmlp_ws.astype(BF16),
        gmlp_bs=jnp.repeat(jnp.swapaxes(gmlp_bs, 1, 2), HEAD_DIM, axis=-1),
        w_out=w_out.astype(BF16),
        ln1_w=row(ln1_w), ln1_b=row(ln1_b),
        ffn_up=ffn_up.astype(BF16),
        ffn_conv_w=ffn_conv_w,
        ffn_conv_b=row(ffn_conv_b),
        ffn_down=ffn_down.astype(BF16),
        ln2_w=row(ln2_w), ln2_b=row(ln2_b),
    )


def _stream_layer(lidx, x, mods, p, states0, is_ctx, full):
    z, gate, small, y_pool, y_gmlp, xbc, q, k, v = _front(lidx, x, mods, p, is_ctx)
    y_f, y_b, ssd_fin = _ssd_scan(lidx, xbc, small, p["dtb_row"], p["ssd_alog_exp"], p["ssd_alog_small"], states0[0])
    o_f, o_b, gdn_fin = _gdn_scan(lidx, q, k, v, small, p["dtb_row"], p["gdn_alog_exp"], states0[1])
    if not full:
        return None, (ssd_fin, gdn_fin)
    x2 = _tail(lidx, x, mods, y_f, y_b, xbc, z, o_f, o_b, gate, y_pool, y_gmlp, p, is_ctx)
    return x2, (ssd_fin, gdn_fin)


def kernel(x, c, ctx, c_ctx, w_mod, b_mod, w_in, ssd_conv_w, ssd_conv_b, ssd_dt_bias, ssd_a_log, ssd_d, ssd_norm_w, pool_w, pool_scale, gdn_conv_w, gdn_dt_bias, gdn_a_log, gdn_norm_w, gmlp_ln_w, gmlp_ln_b, gmlp_ws, gmlp_bs, w_out, ln1_w, ln1_b, ffn_up, ffn_conv_w, ffn_conv_b, ffn_down, ln2_w, ln2_b):
    bsz = x.shape[0]
    p = _prep_params(w_in, ssd_conv_w, ssd_conv_b, ssd_dt_bias, ssd_a_log, ssd_d, ssd_norm_w, pool_w, pool_scale,
                     gdn_conv_w, gdn_dt_bias, gdn_a_log, gdn_norm_w, gmlp_ln_w, gmlp_ln_b, gmlp_ws, gmlp_bs, w_out,
                     ln1_w, ln1_b, ffn_up, ffn_conv_w, ffn_conv_b, ffn_down, ln2_w, ln2_b)
    cs = jnp.concatenate([c_ctx[None, :], c, jnp.zeros((8 - 1 - bsz, D_MODEL), F32)], axis=0)
    mods = _modulation(cs, w_mod, b_mod).reshape(DEPTH, 8, 1, 6 * D_MODEL)
    zero_states = (jnp.zeros((bsz, 2, 128, 256), F32), jnp.zeros((bsz, 2, 256, 256), F32))
    lat, cx = x, ctx
    for l in range(DEPTH):
        lidx = jnp.full((1,), l, jnp.int32)
        cx_next, ctx_states = _stream_layer(lidx, cx, mods, p, zero_states, True, l < DEPTH - 1)
        lat, _ = _stream_layer(lidx, lat, mods, p, ctx_states, False, True)
        cx = cx_next
    return lat
```

```python
import functools

import jax
import jax.numpy as jnp
from jax import lax
from jax.experimental import pallas as pl
from jax.experimental.pallas import tpu as pltpu

F32 = jnp.float32
BF16 = jnp.bfloat16

D_MODEL = 1024
DEPTH = 4
D_GROUP = 256
N_HEADS = 4
HEAD_DIM = 64
SSD_CHUNK = 128
GDN_CHUNK = 64
GMLP_CHUNK = 128
GRID_W = 64
POOL_WINDOWS = (2, 4, 8, 16)
D_FF = 2816
CONV_W = 7
FFN_CONV_W = 3
CONV_C = 1280
N_PROJ = 2688
SMALL_W = 128
HALO_F32 = 8
LN_EPS = 1e-6
RMS_EPS = 1e-6
DN_ALPHA = (2 * DEPTH) ** 0.25
NEG_BIG = -1e30
VMEM_LIMIT = 56 * 1024 * 1024


def _iota(shape, dim):
    return lax.broadcasted_iota(jnp.int32, shape, dim)


def _bf(x):
    return x.astype(BF16)


def _dot(a, b):
    return jnp.dot(a, b, preferred_element_type=F32)


def _dot_nt(a, b):
    return lax.dot_general(a, b, (((1,), (1,)), ((), ())), preferred_element_type=F32)


def _dot_tn(a, b):
    return lax.dot_general(a, b, (((0,), (0,)), ((), ())), preferred_element_type=F32)


def _split3(x):
    x1 = _bf(x)
    r1 = x - x1.astype(F32)
    x2 = _bf(r1)
    x3 = _bf(r1 - x2.astype(F32))
    return x1, x2, x3


def _mm_xe(x, e):
    x1, x2, x3 = _split3(x)
    return _dot(x1, e) + _dot(x2, e) + _dot(x3, e)


def _mm_ex(e, x):
    x1, x2, x3 = _split3(x)
    return _dot(e, x1) + _dot(e, x2) + _dot(e, x3)


def _group_sum(x, ones_bd):
    x1 = _bf(x)
    x2 = _bf(x - x1.astype(F32))
    return _dot(x1, ones_bd) + _dot(x2, ones_bd)


def _mm3(a, b):
    ah = _bf(a)
    al = _bf(a - ah.astype(F32))
    bh = _bf(b)
    bl = _bf(b - bh.astype(F32))
    return _dot(ah, bh) + _dot(ah, bl) + _dot(al, bh)


def _sigmoid(x):
    return 0.5 * jnp.tanh(0.5 * x) + 0.5


def _silu(x):
    h = 0.5 * x
    return h + h * jnp.tanh(h)


def _softplus(x):
    return jnp.maximum(x, 0.0) + jnp.log1p(jnp.exp(-jnp.abs(x)))


def _gelu_tanh(x):
    return x * (0.5 * (1.0 + jnp.tanh(0.7978845608028654 * (x + 0.044715 * (x * x * x)))))


def _ln(x):
    mu = jnp.mean(x, axis=-1, keepdims=True)
    xc = x - mu
    var = jnp.mean(xc * xc, axis=-1, keepdims=True)
    return xc * lax.rsqrt(var + LN_EPS)


def _head_ones():
    lane = jnp.arange(D_GROUP) // HEAD_DIM
    return (lane[:, None] == lane[None, :]).astype(BF16)


def _cparams():
    return pltpu.CompilerParams(vmem_limit_bytes=VMEM_LIMIT)


def _resident(shape, index_map):
    return pl.BlockSpec(shape, index_map, pipeline_mode=pl.Buffered(1))


def _mod_kernel(c_ref, w_ref, b_ref, o_ref):
    o_ref[...] = _mm3(_silu(c_ref[...]), w_ref[...]) + b_ref[...]


def _modulation(cs, w_mod, b_mod):
    tn = 768
    n6 = 6 * D_MODEL
    return pl.pallas_call(
        _mod_kernel,
        out_shape=jax.ShapeDtypeStruct((DEPTH, 8, n6), F32),
        grid=(DEPTH, n6 // tn),
        in_specs=[pl.BlockSpec((8, D_MODEL), lambda l, j: (0, 0)),
                  pl.BlockSpec((None, D_MODEL, tn), lambda l, j: (l, 0, j)),
                  pl.BlockSpec((None, 1, tn), lambda l, j: (l, 0, j))],
        out_specs=pl.BlockSpec((None, 8, tn), lambda l, j: (l, 0, j)),
        compiler_params=_cparams(),
        name="modulation",
    )(cs, w_mod, b_mod.reshape(DEPTH, 1, n6))


ROW_TILE = 256


def _mod_spec(is_ctx):
    if is_ctx:
        return pl.BlockSpec((None, None, 1, 6 * D_MODEL), lambda b, i, l: (l[0], 0, 0, 0))
    return pl.BlockSpec((None, None, 1, 6 * D_MODEL), lambda b, i, l: (l[0], b + 1, 0, 0))


def _layer_spec(shape):
    nd = len(shape)
    return _resident((None,) + tuple(shape), lambda b, i, l: (l[0],) + (0,) * nd)


def _conv_window(win, w, width, halo, t):
    n = win.shape[0]
    half = width // 2
    acc = None
    for k in range(width):
        shift = (half - k) % n
        r = pltpu.roll(win, shift, 0) if shift else win
        term = r[halo:halo + t] * w[k:k + 1, :]
        acc = term if acc is None else acc + term
    return acc


def _pool_gmlp(x, uv, pw_ref, ps_ref, lw_ref, lb_ref, ws_ref, bs_ref, yp_ref, ym_ref, row_w):
    t = x.shape[0]
    lane_grp = _iota((t, D_GROUP), 1) >> 6
    ii = _iota((t, t), 0)
    jj = _iota((t, t), 1)
    base = ii - (ii & (row_w - 1))
    pos = ii & (row_w - 1)
    pos_c = _iota((t, D_GROUP), 0) & (row_w - 1)
    pooled = jnp.zeros((t, D_GROUP), F32)
    x_terms = _split3(x)
    for g, w in enumerate(POOL_WINDOWS):
        lo = jnp.maximum(pos - w // 2, 0)
        hi = jnp.minimum(pos + (w - w // 2), row_w)
        band = jnp.where((jj >= base + lo) & (jj < base + hi), 1.0, 0.0).astype(BF16)
        cnt = (jnp.minimum(pos_c + (w - w // 2), row_w) - jnp.maximum(pos_c - w // 2, 0)).astype(F32)
        in_grp = lane_grp == g
        window_sum = sum(_dot(band, jnp.where(in_grp, xt, jnp.zeros((), BF16))) for xt in x_terms)
        pooled = pooled + window_sum / cnt
    yp_ref[...] = _dot(_bf(pooled - x), pw_ref[...]) * ps_ref[...]

    gl = _gelu_tanh(uv)
    u = gl[:, 0:D_GROUP]
    v = _ln(gl[:, D_GROUP:2 * D_GROUP]) * lw_ref[...] + lb_ref[...]
    grp = _iota((GMLP_CHUNK, D_GROUP), 1) >> 6
    for c in range(t // GMLP_CHUNK):
        rows = slice(c * GMLP_CHUNK, (c + 1) * GMLP_CHUNK)
        vc = v[rows]
        acc = bs_ref[...]
        for g in range(4):
            acc = acc + _dot(ws_ref[g], _bf(jnp.where(grp == g, vc, 0.0)))
        ym_ref[rows, :] = u[rows] * acc


def _front_kernel(l_ref, x_ref, mod_ref, w_ref, cw_ref, cb_ref, ones_ref, pw_ref, ps_ref, lw_ref, lb_ref, ws_ref, bs_ref,
                  z_ref, gate_ref, sm_ref, yp_ref, ym_ref, xbc_ref, q_ref, k_ref, v_ref, win_s, *, row_w, lagged):
    del l_ref
    t = x_ref.shape[0]
    halo = HALO_F32
    i = pl.program_id(1)
    n = pl.num_programs(1) - 1
    pad = jnp.zeros((halo, D_GROUP), F32)

    def project():
        mod = mod_ref[...]
        h = _ln(x_ref[...]) * (1.0 + mod[:, D_MODEL:2 * D_MODEL]) + mod[:, 0:D_MODEL]
        p = _dot(_bf(h), w_ref[...])
        z_ref[...] = p[:, 0:256]
        gate_ref[...] = p[:, 1792:2048]
        sm_ref[...] = p[:, 2560:2688]
        _pool_gmlp(p[:, 1536:1792], p[:, 2048:2560], pw_ref, ps_ref, lw_ref, lb_ref, ws_ref, bs_ref, yp_ref, ym_ref,
                   row_w)
        return p[:, 256:256 + CONV_C]

    def conv(window):
        ones_bd = ones_ref[...]
        outs = (None, None, q_ref, k_ref, v_ref)
        for s in range(CONV_C // D_GROUP):
            sl = slice(s * D_GROUP, (s + 1) * D_GROUP)
            y = _silu(_conv_window(window(sl), cw_ref[:, sl], CONV_W, halo, t) + cb_ref[:, sl])
            if s < 2:
                xbc_ref[:, sl] = y
            elif s < 4:
                outs[s][...] = y * lax.rsqrt(_group_sum(y * y, ones_bd) + 1e-6)
            else:
                outs[s][...] = y

    if not lagged:
        cv = project()
        conv(lambda sl: jnp.concatenate([pad, cv[:, sl], pad], axis=0))
        return

    @pl.when(i == 0)
    def _():
        win_s[0:halo, :] = jnp.zeros((halo, CONV_C), F32)
        win_s[halo:halo + t, :] = project()

    @pl.when((i > 0) & (i < n))
    def _():
        cv = project()
        win_s[halo + t:2 * halo + t, :] = cv[0:halo]
        conv(lambda sl: win_s[:, sl])
        win_s[0:halo, :] = win_s[t:t + halo, :]
        win_s[halo:halo + t, :] = cv

    @pl.when(i == n)
    def _():
        win_s[halo + t:2 * halo + t, :] = jnp.zeros((halo, CONV_C), F32)
        conv(lambda sl: win_s[:, sl])


def _tile_specs(t, n):
    if n == 1:
        same = lambda width: pl.BlockSpec((None, t, width), lambda b, i, l: (b, 0, 0))
        return same, same, 1
    cur = lambda width: pl.BlockSpec((None, t, width), lambda b, i, l: (b, jnp.minimum(i, n - 1), 0))
    lag = lambda width: pl.BlockSpec((None, t, width), lambda b, i, l: (b, jnp.maximum(i - 1, 0), 0))
    return cur, lag, n + 1


def _const_spec(shape):
    return _resident(tuple(shape), lambda b, i, l: (0,) * len(shape))


def _front(lidx, x, mods, p, is_ctx):
    bsz, seq, _ = x.shape
    t = min(seq, ROW_TILE)
    n = seq // t
    row_w = seq if is_ctx else GRID_W
    assert not is_ctx or t == seq
    cur, lag, steps = _tile_specs(t, n)
    grid_spec = pltpu.PrefetchScalarGridSpec(
        num_scalar_prefetch=1,
        grid=(bsz, steps),
        in_specs=[cur(D_MODEL), _mod_spec(is_ctx), _layer_spec((D_MODEL, N_PROJ)), _layer_spec((CONV_W, CONV_C)),
                  _layer_spec((1, CONV_C)), _const_spec((256, 256)),
                  _layer_spec((256, 256)), _layer_spec((1, 256)), _layer_spec((1, 256)),
                  _layer_spec((1, 256)), _layer_spec((4, GMLP_CHUNK, GMLP_CHUNK)), _layer_spec((GMLP_CHUNK, 256))],
        out_specs=[cur(256), cur(256), cur(SMALL_W), cur(256), cur(256), lag(512), lag(256), lag(256), lag(256)],
        scratch_shapes=[pltpu.VMEM((t + 2 * HALO_F32, CONV_C), F32)],
    )
    widths = (256, 256, SMALL_W, 256, 256, 512, 256, 256, 256)
    return pl.pallas_call(
        functools.partial(_front_kernel, row_w=row_w, lagged=n > 1),
        out_shape=[jax.ShapeDtypeStruct((bsz, seq, w), F32) for w in widths],
        grid_spec=grid_spec,
        compiler_params=_cparams(),
        name="front",
    )(lidx, x, mods, p["w_in"], p["conv_w"], p["conv_b"], _head_ones(), p["pool_wbd"], p["pool_scale"],
      p["gmlp_ln_w"], p["gmlp_ln_b"], p["gmlp_ws"], p["gmlp_bs"])


SCAN_GROUP = 8


def _scan_specs(tb, nb):
    fwd = lambda width: pl.BlockSpec((None, tb, width), lambda b, i, l: (b, i, 0))
    bwd = lambda width: pl.BlockSpec((None, tb, width), lambda b, i, l: (b, nb - 1 - i, 0))
    par = lambda shape: _resident((None,) + tuple(shape), lambda b, i, l: (l[0],) + (0,) * len(shape))
    state = lambda rows: pl.BlockSpec((None, 2, rows, 256), lambda b, i, l: (b, 0, 0, 0))
    return fwd, bwd, par, state


def _groups(tasks):
    return [tasks[s:s + SCAN_GROUP] for s in range(0, len(tasks), SCAN_GROUP)]


def _ssd_kernel(l_ref, xf_ref, smf_ref, xb_ref, smb_ref, dtb_ref, alog_ref, asm_ref, h0_ref, yf_ref, yb_ref, hf_ref,
                s_ref, yi_s, ec_s, upd_s, el_s, *, nc):
    del l_ref
    q = SSD_CHUNK
    i = pl.program_id(1)

    @pl.when(i == 0)
    def _():
        s_ref[...] = h0_ref[...]

    x_refs = (xf_ref, xb_ref)
    sm_refs = (smf_ref, smb_ref)
    y_refs = (yf_ref, yb_ref)
    sign = (1, -1)
    dif = _iota((q, q), 0) - _iota((q, q), 1)
    dif4 = _iota((q, 4 * q), 0) - (_iota((q, 4 * q), 1) & (q - 1))
    tri_bf = [jnp.where(dif * s >= 0, 1.0, 0.0).astype(BF16) for s in sign]
    tri4 = [dif4 * s >= 0 for s in sign]
    src = _iota((SMALL_W, D_GROUP), 0)
    expand = [jnp.where(src == 4 * d + (_iota((SMALL_W, D_GROUP), 1) >> 6), 1.0, 0.0).astype(BF16) for d in range(2)]
    src4 = _iota((SMALL_W, 4 * q), 0)
    expand4 = [jnp.where(src4 == 4 * d + (_iota((SMALL_W, 4 * q), 1) >> 7), 1.0, 0.0).astype(BF16) for d in range(2)]
    a_row = [-jnp.exp(alog_ref[d]) for d in range(2)]
    a_small = -jnp.exp(asm_ref[...])
    lane_c = _iota((q, q), 1) >> 6
    head_rows = (_iota((4 * q, D_GROUP), 0) >> 7) == (_iota((4 * q, D_GROUP), 1) >> 6)
    state_mask = (_iota((q, D_GROUP), 0) >> 6) == (_iota((q, D_GROUP), 1) >> 7)

    dt_exp, dta, dts = [], [], []
    for d in range(2):
        dt_all = _softplus(sm_refs[d][...] + dtb_ref[...])
        e = _mm_xe(dt_all, expand[d])
        dt_exp.append(e)
        dta.append(e * a_row[d])
        dts.append(dt_all * a_small)

    for tasks in _groups([(d, c) for c in range(nc) for d in range(2)]):
        rows = [slice(c * q, (c + 1) * q) for _, c in tasks]
        cum = [_mm_ex(tri_bf[d], dta[d][r]) for (d, _), r in zip(tasks, rows)]
        cums = [_mm_ex(tri_bf[d], dts[d][r]) for (d, _), r in zip(tasks, rows)]
        col4 = [_mm_xe(cs, expand4[d]) for (d, _), cs in zip(tasks, cums)]
        cum_t = [cs.T for cs in cums]
        row4 = [jnp.concatenate([ct[4 * d + h:4 * d + h + 1, :] for h in range(N_HEADS)], axis=1)
                for (d, _), ct in zip(tasks, cum_t)]
        decay = [jnp.exp(jnp.where(tri4[d], c4 - r4, NEG_BIG)) for (d, _), c4, r4 in zip(tasks, col4, row4)]
        xc = [x_refs[d][r, :] for (d, _), r in zip(tasks, rows)]
        bm_bf = [_bf(x[:, 256:384]) for x in xc]
        gram = [[_dot_nt(_bf(jnp.where(lane_c == g, x[:, 384:512], 0.0)), b) for g in range(2)]
                for x, b in zip(xc, bm_bf)]
        xdt = [x[:, 0:256] * dt_exp[d][r] for (d, _), x, r in zip(tasks, xc, rows)]
        x_bd = [jnp.where(head_rows, jnp.concatenate([_bf(v)] * N_HEADS, axis=0), jnp.zeros((), BF16)) for v in xdt]
        scores = [jnp.concatenate([g[0], g[0], g[1], g[1]], axis=1) * dc for g, dc in zip(gram, decay)]
        y_intra = [_dot(_bf(sc), xb) for sc, xb in zip(scores, x_bd)]
        last = [cm[q - 1:q, :] if d == 0 else cm[0:1, :] for (d, _), cm in zip(tasks, cum)]
        upd = [_dot_tn(b, _bf(v * jnp.exp(la - cm))) for b, v, la, cm in zip(bm_bf, xdt, last, cum)]
        for t, (d, c) in enumerate(tasks):
            yi_s[d, rows[t], :] = y_intra[t]
            ec_s[d, rows[t], :] = jnp.exp(cum[t])
            upd_s[d, c] = jnp.where(state_mask, upd[t], 0.0)
            el_s[d, c] = jnp.exp(last[t])

    for j in range(nc):
        for d in range(2):
            c = j if d == 0 else nc - 1 - j
            r = slice(c * q, (c + 1) * q)
            s = s_ref[d]
            y_refs[d][r, :] = _dot(_bf(x_refs[d][r, 384:512]), _bf(s)) * ec_s[d, r, :] + yi_s[d, r, :]
            s_ref[d] = s * el_s[d, c] + upd_s[d, c]

    @pl.when(i == pl.num_programs(1) - 1)
    def _():
        hf_ref[...] = s_ref[...]


def _ssd_scan(lidx, xbc, small, dtb_row, alog_exp, alog_small, h0):
    bsz, seq, _ = xbc.shape
    tb = min(seq, 512)
    nb = seq // tb
    nc = tb // SSD_CHUNK
    fwd, bwd, par, state = _scan_specs(tb, nb)
    grid_spec = pltpu.PrefetchScalarGridSpec(
        num_scalar_prefetch=1,
        grid=(bsz, nb),
        in_specs=[fwd(512), fwd(SMALL_W), bwd(512), bwd(SMALL_W), par((1, SMALL_W)), par((2, 1, 256)),
                  par((1, SMALL_W)), state(128)],
        out_specs=[fwd(256), bwd(256), state(128)],
        scratch_shapes=[pltpu.VMEM((2, 128, 256), F32), pltpu.VMEM((2, tb, 256), F32), pltpu.VMEM((2, tb, 256), F32),
                        pltpu.VMEM((2, nc, 128, 256), F32), pltpu.VMEM((2, nc, 1, 256), F32)],
    )
    return pl.pallas_call(
        functools.partial(_ssd_kernel, nc=nc),
        out_shape=[jax.ShapeDtypeStruct((bsz, seq, 256), F32), jax.ShapeDtypeStruct((bsz, seq, 256), F32),
                   jax.ShapeDtypeStruct((bsz, 2, 128, 256), F32)],
        grid_spec=grid_spec,
        compiler_params=_cparams(),
        name="ssd_scan",
    )(lidx, xbc, small, xbc, small, dtb_row, alog_exp, alog_small, h0)


def _gdn_kernel(l_ref, qf_ref, kf_ref, vf_ref, smf_ref, qb_ref, kb_ref, vb_ref, smb_ref, dtb_ref, alog_ref, s0_ref,
                of_ref, ob_ref, sf_ref, s_ref, u_s, kt_s, wq_s, at_s, gl_s, *, nck):
    del l_ref
    cl = GDN_CHUNK
    i = pl.program_id(1)

    @pl.when(i == 0)
    def _():
        s_ref[...] = s0_ref[...]

    q_refs = (qf_ref, qb_ref)
    k_refs = (kf_ref, kb_ref)
    v_refs = (vf_ref, vb_ref)
    sm_refs = (smf_ref, smb_ref)
    o_refs = (of_ref, ob_ref)
    sign = (1, -1)
    ii = _iota((cl, D_GROUP), 0)
    jj = _iota((cl, D_GROUP), 1) & (cl - 1)
    low = [(ii - jj) * s >= 0 for s in sign]
    strict = [(ii - jj) * s > 0 for s in sign]
    diag = ii == jj
    blk16 = (ii >> 4) == (jj >> 4)
    blk32 = (ii >> 5) == (jj >> 5)
    off_masks = (blk32 & jnp.logical_not(blk16), jnp.logical_not(blk32))
    head_bd = (_iota((D_GROUP, D_GROUP), 0) >> 6) == (_iota((D_GROUP, D_GROUP), 1) >> 6)
    dif = _iota((cl, cl), 0) - _iota((cl, cl), 1)
    tri_bf = [jnp.where(dif * s >= 0, 1.0, 0.0).astype(BF16) for s in sign]
    src = _iota((SMALL_W, 2 * D_GROUP), 0)
    lane = _iota((SMALL_W, 2 * D_GROUP), 1)
    expand = [jnp.where(src == 8 + 8 * (lane >> 8) + 4 * d + ((lane >> 6) & 3), 1.0, 0.0).astype(BF16)
              for d in range(2)]
    a_row = [-jnp.exp(alog_ref[d]) for d in range(2)]

    def blockdiag(y_bf):
        return jnp.where(head_bd, jnp.concatenate([y_bf] * N_HEADS, axis=0), jnp.zeros((), BF16))

    def mul(xs, ys):
        return [_dot(_bf(x), blockdiag(_bf(y))) for x, y in zip(xs, ys)]

    g_exp, b_exp = [], []
    for d in range(2):
        sm = sm_refs[d][...]
        lane_s = _iota(sm.shape, 1)
        t = jnp.where((lane_s >= 8) & (lane_s < 16), _softplus(sm + dtb_ref[...]), _sigmoid(sm))
        e = _mm_xe(t, expand[d])
        g_exp.append(e[:, 0:D_GROUP] * a_row[d])
        b_exp.append(e[:, D_GROUP:2 * D_GROUP])

    def chunk_local(tasks):
        ds = [d for d, _ in tasks]
        rows = [slice(c * cl, (c + 1) * cl) for _, c in tasks]
        gc = [_mm_ex(tri_bf[d], g_exp[d][r]) for d, r in zip(ds, rows)]
        gc_row = [jnp.sum(jnp.where(diag, g, 0.0), axis=0, keepdims=True) for g in gc]
        decay = [jnp.exp(jnp.where(low[d], g - gr, NEG_BIG)) for d, g, gr in zip(ds, gc, gc_row)]
        yield
        kc = [k_refs[d][r, :] for d, r in zip(ds, rows)]
        qs = [q_refs[d][r, :] * (HEAD_DIM ** -0.5) for d, r in zip(ds, rows)]
        be = [b_exp[d][r] for d, r in zip(ds, rows)]
        kb = [k * b for k, b in zip(kc, be)]
        k_bd = [blockdiag(_bf(k)) for k in kc]
        kq = [_dot_nt(_bf(jnp.concatenate([x, y], axis=0)), kd) for x, y, kd in zip(kb, qs, k_bd)]
        a = [jnp.where(strict[d], x[0:cl] * dc, 0.0) for d, x, dc in zip(ds, kq, decay)]
        attn = [x[cl:2 * cl] * dc for x, dc in zip(kq, decay)]
        yield
        ad = [jnp.where(blk16, x, 0.0) for x in a]
        a2 = mul(ad, ad)
        yield
        a4 = mul(a2, a2)
        yield
        a8 = mul(a4, a4)
        yield
        m = [-x for x in ad]
        for p in (a2, a4, a8):
            m = [mm + pp + mp for mm, pp, mp in zip(m, p, mul(m, p))]
            yield
        for off in off_masks:
            o = [jnp.where(off, x, 0.0) for x in a]
            x = [oo + mo for oo, mo in zip(o, mul(m, o))]
            yield
            m = [mm - (xx + xm) for mm, xx, xm in zip(m, x, mul(x, m))]
            yield
        rv = [v_refs[d][r, :] * b for d, r, b in zip(ds, rows, be)]
        rw = [x * jnp.exp(g) for x, g in zip(kb, gc)]
        u = [x + mx for x, mx in zip(rv, mul(m, rv))]
        yield
        w = [x + mx for x, mx in zip(rw, mul(m, rw))]
        for t, (d, c) in enumerate(tasks):
            g_last = gc[t][cl - 1:cl, :] if d == 0 else gc[t][0:1, :]
            u_s[d, rows[t], :] = u[t]
            kt_s[d, rows[t], :] = kc[t] * jnp.exp(g_last - gc[t])
            wq_s[d, c, 0:cl, :] = w[t]
            wq_s[d, c, cl:2 * cl, :] = qs[t] * jnp.exp(gc[t])
            at_s[d, rows[t], :] = attn[t]
            gl_s[d, c] = jnp.exp(g_last)
        yield

    def recurrence(steps):
        for j in steps:
            cs = [j, nck - 1 - j]
            rows = [slice(c * cl, (c + 1) * cl) for c in cs]
            s = [s_ref[d] for d in range(2)]
            ws = [_dot(_bf(wq_s[d, cs[d]]), _bf(s[d])) for d in range(2)]
            yield
            v_bf = [_bf(u_s[d, rows[d], :] - ws[d][0:cl]) for d in range(2)]
            upd = [_dot_tn(_bf(kt_s[d, rows[d], :]), v_bf[d]) for d in range(2)]
            o = [_dot(_bf(at_s[d, rows[d], :]), blockdiag(v_bf[d])) + ws[d][cl:2 * cl] for d in range(2)]
            yield
            for d in range(2):
                o_refs[d][rows[d], :] = o[d]
                s_ref[d] = s[d] * gl_s[d, cs[d]] + jnp.where(head_bd, upd[d], 0.0)
            yield

    def interleave(*gens):
        live = list(gens)
        while live:
            live = [g for g in live if next(g, StopIteration) is not StopIteration]

    lead = nck // 2 if 2 * nck > SCAN_GROUP else nck
    first = [(0, c) for c in range(lead)] + [(1, c) for c in range(nck - 1, nck - 1 - lead, -1)]
    second = [(0, c) for c in range(lead, nck)] + [(1, c) for c in range(nck - 1 - lead, -1, -1)]
    interleave(chunk_local(first))
    if second:
        interleave(chunk_local(second), recurrence(range(lead)))
        interleave(recurrence(range(lead, nck)))
    else:
        interleave(recurrence(range(nck)))

    @pl.when(i == pl.num_programs(1) - 1)
    def _():
        sf_ref[...] = s_ref[...]


def _gdn_scan(lidx, q, k, v, small, dtb_row, alog_exp, s0):
    bsz, seq, _ = q.shape
    tb = min(seq, 512)
    nb = seq // tb
    nck = tb // GDN_CHUNK
    fwd, bwd, par, state = _scan_specs(tb, nb)
    grid_spec = pltpu.PrefetchScalarGridSpec(
        num_scalar_prefetch=1,
        grid=(bsz, nb),
        in_specs=[fwd(256), fwd(256), fwd(256), fwd(SMALL_W), bwd(256), bwd(256), bwd(256), bwd(SMALL_W),
                  par((1, SMALL_W)), par((2, 1, 256)), state(256)],
        out_specs=[fwd(256), bwd(256), state(256)],
        scratch_shapes=[pltpu.VMEM((2, 256, 256), F32), pltpu.VMEM((2, tb, 256), F32), pltpu.VMEM((2, tb, 256), F32),
                        pltpu.VMEM((2, nck, 2 * GDN_CHUNK, 256), F32), pltpu.VMEM((2, tb, 256), F32),
                        pltpu.VMEM((2, nck, 1, 256), F32)],
    )
    return pl.pallas_call(
        functools.partial(_gdn_kernel, nck=nck),
        out_shape=[jax.ShapeDtypeStruct((bsz, seq, 256), F32), jax.ShapeDtypeStruct((bsz, seq, 256), F32),
                   jax.ShapeDtypeStruct((bsz, 2, 256, 256), F32)],
        grid_spec=grid_spec,
        compiler_params=_cparams(),
        name="gdn_scan",
    )(lidx, q, k, v, small, q, k, v, small, dtb_row, alog_exp, s0)


FFN_STRIP = 256
EDGE = 2 * HALO_F32


def _tail_kernel(l_ref, x_ref, mod_ref, yf_ref, yb_ref, xs_ref, z_ref, of_ref, ob_ref, gate_ref, yp_ref, ym_ref,
                 dsk_ref, snw_ref, gnw_ref, ones_ref, wo_ref, l1w_ref, l1b_ref, wup_ref, cw_ref, cb_ref, wd_ref, l2w_ref,
                 l2b_ref, x2_ref, win_s, b_s, x1_s, *, lagged):
    del l_ref
    dm = D_MODEL
    halo = HALO_F32
    t = x_ref.shape[0]
    i = pl.program_id(1)
    n = pl.num_programs(1) - 1
    n_strips = D_FF // FFN_STRIP
    pad = jnp.zeros((halo, FFN_STRIP), F32)

    def half_layer():
        mod = mod_ref[...]
        ys = (yf_ref[...] + yb_ref[...] + xs_ref[...] * dsk_ref[...]) * _silu(z_ref[...])
        ys = ys * lax.rsqrt(jnp.mean(ys * ys, axis=-1, keepdims=True) + RMS_EPS) * snw_ref[...]
        og = of_ref[...] + ob_ref[...]
        ms = _group_sum(og * og, ones_ref[...]) * (1.0 / HEAD_DIM)
        og = og * lax.rsqrt(ms + RMS_EPS) * gnw_ref[...] * _silu(gate_ref[...])
        proj = (_dot(_bf(yp_ref[...]), wo_ref[256:512, :]) + _dot(_bf(ym_ref[...]), wo_ref[768:1024, :])
                + _dot(_bf(ys), wo_ref[0:256, :]) + _dot(_bf(og), wo_ref[512:768, :]))
        x1 = _ln(DN_ALPHA * x_ref[...] + mod[:, 2 * dm:3 * dm] * proj) * l1w_ref[...] + l1b_ref[...]
        return x1, _bf(_ln(x1) * (1.0 + mod[:, 4 * dm:5 * dm]) + mod[:, 3 * dm:4 * dm])

    def up_strips(h2):
        up = lambda j: (_dot(h2, wup_ref[:, j * FFN_STRIP:(j + 1) * FFN_STRIP]),
                        _dot(h2, wup_ref[:, D_FF + j * FFN_STRIP:D_FF + (j + 1) * FFN_STRIP]))
        ab = up(0)
        for j in range(n_strips):
            ab_next = up(j + 1) if j + 1 < n_strips else None
            yield ab
            ab = ab_next

    def second_residual(x1, f):
        x2_ref[...] = _ln(DN_ALPHA * x1 + mod_ref[:, 5 * dm:6 * dm] * f) * l2w_ref[...] + l2b_ref[...]

    def ffn(produce, finish, first=False):
        if produce:
            x1, h2 = half_layer()
            strips = up_strips(h2)
        f = jnp.zeros((t, dm), F32)
        for j in range(n_strips):
            cols = slice(j * FFN_STRIP, (j + 1) * FFN_STRIP)
            if produce:
                a, b = next(strips)
            if finish:
                w = cw_ref[:, cols]
                head = a[0:halo] if produce else pad
                edge_win = jnp.concatenate([win_s[t - EDGE:t + halo, cols], head], axis=0)
                conv = jnp.concatenate([_conv_window(win_s[:, cols], w, FFN_CONV_W, halo, t - EDGE),
                                        _conv_window(edge_win, w, FFN_CONV_W, halo, EDGE)], axis=0)
                f = f + _dot(_bf(_silu(conv + cb_ref[:, cols]) * b_s[:, cols]), wd_ref[cols, :])
            if produce:
                win_s[0:halo, cols] = pad if first else win_s[t:t + halo, cols]
                win_s[halo:halo + t, cols] = a
                b_s[:, cols] = b
        if finish:
            second_residual(x1_s[...], f)
        if produce:
            x1_s[...] = x1

    if not lagged:
        x1, h2 = half_layer()
        f = jnp.zeros((t, dm), F32)
        for j, (a, b) in enumerate(up_strips(h2)):
            cols = slice(j * FFN_STRIP, (j + 1) * FFN_STRIP)
            conv = _conv_window(jnp.concatenate([pad, a, pad], axis=0), cw_ref[:, cols], FFN_CONV_W, halo, t)
            f = f + _dot(_bf(_silu(conv + cb_ref[:, cols]) * b), wd_ref[cols, :])
        second_residual(x1, f)
        return

    pl.when(i == 0)(lambda: ffn(True, False, first=True))
    pl.when((i > 0) & (i < n))(lambda: ffn(True, True))
    pl.when(i == n)(lambda: ffn(False, True))


def _tail(lidx, x, mods, y_f, y_b, xbc, z, o_f, o_b, gate, y_pool, y_gmlp, p, is_ctx):
    bsz, seq, _ = x.shape
    t = min(seq, ROW_TILE)
    n = seq // t
    cur, lag, steps = _tile_specs(t, n)
    grid_spec = pltpu.PrefetchScalarGridSpec(
        num_scalar_prefetch=1,
        grid=(bsz, steps),
        in_specs=[cur(D_MODEL), _mod_spec(is_ctx)] + [cur(256)] * 9
                 + [_layer_spec((1, 256)), _layer_spec((1, 256)), _layer_spec((1, 256)), _const_spec((256, 256)),
                    _layer_spec((D_MODEL, D_MODEL)), _layer_spec((1, D_MODEL)), _layer_spec((1, D_MODEL)),
                    _layer_spec((D_MODEL, 2 * D_FF)), _layer_spec((FFN_CONV_W, D_FF)), _layer_spec((1, D_FF)),
                    _layer_spec((D_FF, D_MODEL)), _layer_spec((1, D_MODEL)), _layer_spec((1, D_MODEL))],
        out_specs=lag(D_MODEL),
        scratch_shapes=[pltpu.VMEM((t + HALO_F32, D_FF), F32), pltpu.VMEM((t, D_FF), F32),
                        pltpu.VMEM((t, D_MODEL), F32)],
    )
    return pl.pallas_call(
        functools.partial(_tail_kernel, lagged=n > 1),
        out_shape=jax.ShapeDtypeStruct((bsz, seq, D_MODEL), F32),
        grid_spec=grid_spec,
        compiler_params=_cparams(),
        name="tail",
    )(lidx, x, mods, y_f, y_b, xbc, z, o_f, o_b, gate, y_pool, y_gmlp, p["d_exp"], p["ssd_nw"], p["gdn_nw"],
      _head_ones(), p["w_out"], p["ln1_w"], p["ln1_b"], p["ffn_up"], p["ffn_conv_w"], p["ffn_conv_b"], p["ffn_down"],
      p["ln2_w"], p["ln2_b"])


def _prep_params(w_in, ssd_conv_w, ssd_conv_b, ssd_dt_bias, ssd_a_log, ssd_d, ssd_norm_w, pool_w, pool_scale,
                 gdn_conv_w, gdn_dt_bias, gdn_a_log, gdn_norm_w, gmlp_ln_w, gmlp_ln_b, gmlp_ws, gmlp_bs, w_out,
                 ln1_w, ln1_b, ffn_up, ffn_conv_w, ffn_conv_b, ffn_down, ln2_w, ln2_b):
    nl = DEPTH
    cols = [w_in[:, :, 0:256], w_in[:, :, 256:768], w_in[:, :, 1032:1800], w_in[:, :, 776:1032],
            w_in[:, :, 1800:2056], w_in[:, :, 2072:2584], w_in[:, :, 768:776], w_in[:, :, 2056:2072],
            jnp.zeros((nl, D_MODEL, SMALL_W - 24), w_in.dtype)]
    row = lambda t: t.reshape(nl, 1, -1)
    pad_small = lambda t: jnp.concatenate([t, jnp.zeros((nl, SMALL_W - t.shape[-1]), F32)], axis=-1).reshape(nl, 1, SMALL_W)
    eye4 = jnp.eye(4, dtype=F32)
    pool_bd = (eye4[None, :, None, :, None] * pool_w[:, :, :, None, :]).reshape(nl, 256, 256)
    return dict(
        w_in=jnp.concatenate(cols, axis=-1).astype(BF16),
        conv_w=jnp.concatenate([ssd_conv_w, gdn_conv_w], axis=-1),
        conv_b=jnp.concatenate([ssd_conv_b, jnp.zeros((nl, 768), F32)], axis=-1).reshape(nl, 1, CONV_C),
        dtb_row=pad_small(jnp.concatenate([ssd_dt_bias.reshape(nl, 8), gdn_dt_bias.reshape(nl, 8)], axis=-1)),
        ssd_alog_exp=jnp.repeat(ssd_a_log, HEAD_DIM, axis=-1).reshape(nl, 2, 1, 256),
        ssd_alog_small=pad_small(ssd_a_log.reshape(nl, 8)),
        gdn_alog_exp=jnp.repeat(gdn_a_log, HEAD_DIM, axis=-1).reshape(nl, 2, 1, 256),
        d_exp=row(jnp.repeat(ssd_d, HEAD_DIM, axis=-1)),
        ssd_nw=row(ssd_norm_w),
        gdn_nw=row(jnp.tile(gdn_norm_w, (1, N_HEADS))),
        pool_wbd=pool_bd.astype(BF16),
        pool_scale=row(pool_scale),
        gmlp_ln_w=row(gmlp_ln_w),
        gmlp_ln_b=row(gmlp_ln_b),
        gmlp_ws=gmlp_ws.astype(BF16),
        gmlp_bs=jnp.repeat(jnp.swapaxes(gmlp_bs, 1, 2), HEAD_DIM, axis=-1),
        w_out=w_out.astype(BF16),
        ln1_w=row(ln1_w), ln1_b=row(ln1_b),
        ffn_up=ffn_up.astype(BF16),
        ffn_conv_w=ffn_conv_w,
        ffn_conv_b=row(ffn_conv_b),
        ffn_down=ffn_down.astype(BF16),
        ln2_w=row(ln2_w), ln2_b=row(ln2_b),
    )


def _stream_layer(lidx, x, mods, p, states0, is_ctx, full):
    z, gate, small, y_pool, y_gmlp, xbc, q, k, v = _front(lidx, x, mods, p, is_ctx)
    y_f, y_b, ssd_fin = _ssd_scan(lidx, xbc, small, p["dtb_row"], p["ssd_alog_exp"], p["ssd_alog_small"], states0[0])
    o_f, o_b, gdn_fin = _gdn_scan(lidx, q, k, v, small, p["dtb_row"], p["gdn_alog_exp"], states0[1])
    if not full:
        return None, (ssd_fin, gdn_fin)
    x2 = _tail(lidx, x, mods, y_f, y_b, xbc, z, o_f, o_b, gate, y_pool, y_gmlp, p, is_ctx)
    return x2, (ssd_fin, gdn_fin)


def kernel(x, c, ctx, c_ctx, w_mod, b_mod, w_in, ssd_conv_w, ssd_conv_b, ssd_dt_bias, ssd_a_log, ssd_d, ssd_norm_w, pool_w, pool_scale, gdn_conv_w, gdn_dt_bias, gdn_a_log, gdn_norm_w, gmlp_ln_w, gmlp_ln_b, gmlp_ws, gmlp_bs, w_out, ln1_w, ln1_b, ffn_up, ffn_conv_w, ffn_conv_b, ffn_down, ln2_w, ln2_b):
    bsz = x.shape[0]
    p = _prep_params(w_in, ssd_conv_w, ssd_conv_b, ssd_dt_bias, ssd_a_log, ssd_d, ssd_norm_w, pool_w, pool_scale,
                     gdn_conv_w, gdn_dt_bias, gdn_a_log, gdn_norm_w, gmlp_ln_w, gmlp_ln_b, gmlp_ws, gmlp_bs, w_out,
                     ln1_w, ln1_b, ffn_up, ffn_conv_w, ffn_conv_b, ffn_down, ln2_w, ln2_b)
    cs = jnp.concatenate([c_ctx[None, :], c, jnp.zeros((8 - 1 - bsz, D_MODEL), F32)], axis=0)
    mods = _modulation(cs, w_mod, b_mod).reshape(DEPTH, 8, 1, 6 * D_MODEL)
    zero_states = (jnp.zeros((bsz, 2, 128, 256), F32), jnp.zeros((bsz, 2, 256, 256), F32))
    lat, cx = x, ctx
    for l in range(DEPTH):
        lidx = jnp.full((1,), l, jnp.int32)
        cx_next, ctx_states = _stream_layer(lidx, cx, mods, p, zero_states, True, l < DEPTH - 1)
        lat, _ = _stream_layer(lidx, lat, mods, p, ctx_states, False, True)
        cx = cx_next
    return lat
```

```python
import functools

import jax
import jax.numpy as jnp
from jax import lax
from jax.experimental import pallas as pl
from jax.experimental.pallas import tpu as pltpu

F32 = jnp.float32
BF16 = jnp.bfloat16

D_MODEL = 1024
DEPTH = 4
D_GROUP = 256
N_HEADS = 4
HEAD_DIM = 64
SSD_CHUNK = 128
GDN_CHUNK = 64
GMLP_CHUNK = 128
GRID_W = 64
POOL_WINDOWS = (2, 4, 8, 16)
D_FF = 2816
CONV_W = 7
FFN_CONV_W = 3
CONV_C = 1280
N_PROJ = 2688
SMALL_W = 128
HALO_F32 = 8
LN_EPS = 1e-6
RMS_EPS = 1e-6
DN_ALPHA = (2 * DEPTH) ** 0.25
NEG_BIG = -1e30
VMEM_LIMIT = 56 * 1024 * 1024


def _iota(shape, dim):
    return lax.broadcasted_iota(jnp.int32, shape, dim)


def _bf(x):
    return x.astype(BF16)


def _dot(a, b):
    return jnp.dot(a, b, preferred_element_type=F32)


def _dot_nt(a, b):
    return lax.dot_general(a, b, (((1,), (1,)), ((), ())), preferred_element_type=F32)


def _dot_tn(a, b):
    return lax.dot_general(a, b, (((0,), (0,)), ((), ())), preferred_element_type=F32)


def _split3(x):
    x1 = _bf(x)
    r1 = x - x1.astype(F32)
    x2 = _bf(r1)
    x3 = _bf(r1 - x2.astype(F32))
    return x1, x2, x3


def _mm_xe(x, e):
    x1, x2, x3 = _split3(x)
    return _dot(x1, e) + _dot(x2, e) + _dot(x3, e)


def _mm_ex(e, x):
    x1, x2, x3 = _split3(x)
    return _dot(e, x1) + _dot(e, x2) + _dot(e, x3)


def _split2(x):
    x1 = _bf(x)
    return x1, _bf(x - x1.astype(F32))


def _mm_xe2(x, e):
    x1, x2 = _split2(x)
    return _dot(x1, e) + _dot(x2, e)


def _group_sum(x, ones_bd):
    return _mm_xe2(x, ones_bd)


def _mm3(a, b):
    ah = _bf(a)
    al = _bf(a - ah.astype(F32))
    bh = _bf(b)
    bl = _bf(b - bh.astype(F32))
    return _dot(ah, bh) + _dot(ah, bl) + _dot(al, bh)


def _sigmoid(x):
    return 0.5 * jnp.tanh(0.5 * x) + 0.5


def _silu(x):
    h = 0.5 * x
    return h + h * jnp.tanh(h)


def _softplus(x):
    return jnp.maximum(x, 0.0) + jnp.log1p(jnp.exp(-jnp.abs(x)))


def _gelu_tanh(x):
    return x * (0.5 * (1.0 + jnp.tanh(0.7978845608028654 * (x + 0.044715 * (x * x * x)))))


def _ln(x):
    mu = jnp.mean(x, axis=-1, keepdims=True)
    xc = x - mu
    var = jnp.mean(xc * xc, axis=-1, keepdims=True)
    return xc * lax.rsqrt(var + LN_EPS)


def _head_ones():
    lane = jnp.arange(D_GROUP) // HEAD_DIM
    return (lane[:, None] == lane[None, :]).astype(BF16)


def _cparams():
    return pltpu.CompilerParams(vmem_limit_bytes=VMEM_LIMIT)


def _resident(shape, index_map):
    return pl.BlockSpec(shape, index_map, pipeline_mode=pl.Buffered(1))


def _mod_kernel(c_ref, w_ref, b_ref, o_ref):
    o_ref[...] = _mm3(_silu(c_ref[...]), w_ref[...]) + b_ref[...]


def _modulation(cs, w_mod, b_mod):
    tn = 768
    n6 = 6 * D_MODEL
    return pl.pallas_call(
        _mod_kernel,
        out_shape=jax.ShapeDtypeStruct((DEPTH, 8, n6), F32),
        grid=(DEPTH, n6 // tn),
        in_specs=[pl.BlockSpec((8, D_MODEL), lambda l, j: (0, 0)),
                  pl.BlockSpec((None, D_MODEL, tn), lambda l, j: (l, 0, j)),
                  pl.BlockSpec((None, 1, tn), lambda l, j: (l, 0, j))],
        out_specs=pl.BlockSpec((None, 8, tn), lambda l, j: (l, 0, j)),
        compiler_params=_cparams(),
        name="modulation",
    )(cs, w_mod, b_mod.reshape(DEPTH, 1, n6))


ROW_TILE = 256


def _mod_spec(is_ctx):
    if is_ctx:
        return pl.BlockSpec((None, None, 1, 6 * D_MODEL), lambda b, i, l: (l[0], 0, 0, 0))
    return pl.BlockSpec((None, None, 1, 6 * D_MODEL), lambda b, i, l: (l[0], b + 1, 0, 0))


def _layer_spec(shape):
    nd = len(shape)
    return _resident((None,) + tuple(shape), lambda b, i, l: (l[0],) + (0,) * nd)


def _conv_window(win, w, width, halo, t):
    n = win.shape[0]
    half = width // 2
    acc = None
    for k in range(width):
        shift = (half - k) % n
        r = pltpu.roll(win, shift, 0) if shift else win
        term = r[halo:halo + t] * w[k:k + 1, :]
        acc = term if acc is None else acc + term
    return acc


def _pool_gmlp(x, uv, pw_ref, ps_ref, lw_ref, lb_ref, ws_ref, bs_ref, yp_ref, ym_ref, row_w):
    t = x.shape[0]
    lane_grp = _iota((t, D_GROUP), 1) >> 6
    ii = _iota((t, t), 0)
    jj = _iota((t, t), 1)
    base = ii - (ii & (row_w - 1))
    pos = ii & (row_w - 1)
    pos_c = _iota((t, D_GROUP), 0) & (row_w - 1)
    pooled = jnp.zeros((t, D_GROUP), F32)
    x_terms = _split2(x)
    for g, w in enumerate(POOL_WINDOWS):
        lo = jnp.maximum(pos - w // 2, 0)
        hi = jnp.minimum(pos + (w - w // 2), row_w)
        band = jnp.where((jj >= base + lo) & (jj < base + hi), 1.0, 0.0).astype(BF16)
        cnt = (jnp.minimum(pos_c + (w - w // 2), row_w) - jnp.maximum(pos_c - w // 2, 0)).astype(F32)
        in_grp = lane_grp == g
        window_sum = sum(_dot(band, jnp.where(in_grp, xt, jnp.zeros((), BF16))) for xt in x_terms)
        pooled = pooled + window_sum / cnt
    yp_ref[...] = _dot(_bf(pooled - x), pw_ref[...]) * ps_ref[...]

    gl = _gelu_tanh(uv)
    u = gl[:, 0:D_GROUP]
    v = _ln(gl[:, D_GROUP:2 * D_GROUP]) * lw_ref[...] + lb_ref[...]
    grp = _iota((GMLP_CHUNK, D_GROUP), 1) >> 6
    for c in range(t // GMLP_CHUNK):
        rows = slice(c * GMLP_CHUNK, (c + 1) * GMLP_CHUNK)
        vc = v[rows]
        acc = bs_ref[...]
        for g in range(4):
            acc = acc + _dot(ws_ref[g], _bf(jnp.where(grp == g, vc, 0.0)))
        ym_ref[rows, :] = u[rows] * acc


def _front_kernel(l_ref, x_ref, mod_ref, w_ref, cw_ref, cb_ref, ones_ref, pw_ref, ps_ref, lw_ref, lb_ref, ws_ref, bs_ref,
                  z_ref, gate_ref, sm_ref, yp_ref, ym_ref, xbc_ref, q_ref, k_ref, v_ref, win_s, *, row_w, lagged):
    del l_ref
    t = x_ref.shape[0]
    halo = HALO_F32
    i = pl.program_id(1)
    n = pl.num_programs(1) - 1
    pad = jnp.zeros((halo, D_GROUP), F32)

    def project():
        mod = mod_ref[...]
        h = _ln(x_ref[...]) * (1.0 + mod[:, D_MODEL:2 * D_MODEL]) + mod[:, 0:D_MODEL]
        p = _dot(_bf(h), w_ref[...])
        z_ref[...] = p[:, 0:256]
        gate_ref[...] = p[:, 1792:2048]
        sm_ref[...] = p[:, 2560:2688]
        _pool_gmlp(p[:, 1536:1792], p[:, 2048:2560], pw_ref, ps_ref, lw_ref, lb_ref, ws_ref, bs_ref, yp_ref, ym_ref,
                   row_w)
        return p[:, 256:256 + CONV_C]

    def conv(window):
        ones_bd = ones_ref[...]
        outs = (None, None, q_ref, k_ref, v_ref)
        for s in range(CONV_C // D_GROUP):
            sl = slice(s * D_GROUP, (s + 1) * D_GROUP)
            y = _silu(_conv_window(window(sl), cw_ref[:, sl], CONV_W, halo, t) + cb_ref[:, sl])
            if s < 2:
                xbc_ref[:, sl] = y
            elif s < 4:
                outs[s][...] = y * lax.rsqrt(_group_sum(y * y, ones_bd) + 1e-6)
            else:
                outs[s][...] = y

    if not lagged:
        cv = project()
        conv(lambda sl: jnp.concatenate([pad, cv[:, sl], pad], axis=0))
        return

    @pl.when(i == 0)
    def _():
        win_s[0:halo, :] = jnp.zeros((halo, CONV_C), F32)
        win_s[halo:halo + t, :] = project()

    @pl.when((i > 0) & (i < n))
    def _():
        cv = project()
        win_s[halo + t:2 * halo + t, :] = cv[0:halo]
        conv(lambda sl: win_s[:, sl])
        win_s[0:halo, :] = win_s[t:t + halo, :]
        win_s[halo:halo + t, :] = cv

    @pl.when(i == n)
    def _():
        win_s[halo + t:2 * halo + t, :] = jnp.zeros((halo, CONV_C), F32)
        conv(lambda sl: win_s[:, sl])


def _tile_specs(t, n):
    if n == 1:
        same = lambda width: pl.BlockSpec((None, t, width), lambda b, i, l: (b, 0, 0))
        return same, same, 1
    cur = lambda width: pl.BlockSpec((None, t, width), lambda b, i, l: (b, jnp.minimum(i, n - 1), 0))
    lag = lambda width: pl.BlockSpec((None, t, width), lambda b, i, l: (b, jnp.maximum(i - 1, 0), 0))
    return cur, lag, n + 1


def _const_spec(shape):
    return _resident(tuple(shape), lambda b, i, l: (0,) * len(shape))


def _front(lidx, x, mods, p, is_ctx):
    bsz, seq, _ = x.shape
    t = min(seq, ROW_TILE)
    n = seq // t
    row_w = seq if is_ctx else GRID_W
    assert not is_ctx or t == seq
    cur, lag, steps = _tile_specs(t, n)
    grid_spec = pltpu.PrefetchScalarGridSpec(
        num_scalar_prefetch=1,
        grid=(bsz, steps),
        in_specs=[cur(D_MODEL), _mod_spec(is_ctx), _layer_spec((D_MODEL, N_PROJ)), _layer_spec((CONV_W, CONV_C)),
                  _layer_spec((1, CONV_C)), _const_spec((256, 256)),
                  _layer_spec((256, 256)), _layer_spec((1, 256)), _layer_spec((1, 256)),
                  _layer_spec((1, 256)), _layer_spec((4, GMLP_CHUNK, GMLP_CHUNK)), _layer_spec((GMLP_CHUNK, 256))],
        out_specs=[cur(256), cur(256), cur(SMALL_W), cur(256), cur(256), lag(512), lag(256), lag(256), lag(256)],
        scratch_shapes=[pltpu.VMEM((t + 2 * HALO_F32, CONV_C), F32)],
    )
    widths = (256, 256, SMALL_W, 256, 256, 512, 256, 256, 256)
    return pl.pallas_call(
        functools.partial(_front_kernel, row_w=row_w, lagged=n > 1),
        out_shape=[jax.ShapeDtypeStruct((bsz, seq, w), F32) for w in widths],
        grid_spec=grid_spec,
        compiler_params=_cparams(),
        name="front",
    )(lidx, x, mods, p["w_in"], p["conv_w"], p["conv_b"], _head_ones(), p["pool_wbd"], p["pool_scale"],
      p["gmlp_ln_w"], p["gmlp_ln_b"], p["gmlp_ws"], p["gmlp_bs"])


SCAN_GROUP = 8


def _scan_specs(tb, nb):
    fwd = lambda width: pl.BlockSpec((None, tb, width), lambda b, i, l: (b, i, 0))
    bwd = lambda width: pl.BlockSpec((None, tb, width), lambda b, i, l: (b, nb - 1 - i, 0))
    par = lambda shape: _resident((None,) + tuple(shape), lambda b, i, l: (l[0],) + (0,) * len(shape))
    state = lambda rows: pl.BlockSpec((None, 2, rows, 256), lambda b, i, l: (b, 0, 0, 0))
    return fwd, bwd, par, state


def _groups(tasks):
    return [tasks[s:s + SCAN_GROUP] for s in range(0, len(tasks), SCAN_GROUP)]


def _ssd_kernel(l_ref, xf_ref, smf_ref, xb_ref, smb_ref, dtb_ref, alog_ref, asm_ref, h0_ref, yf_ref, yb_ref, hf_ref,
                s_ref, yi_s, ec_s, upd_s, el_s, *, nc):
    del l_ref
    q = SSD_CHUNK
    i = pl.program_id(1)

    @pl.when(i == 0)
    def _():
        s_ref[...] = h0_ref[...]

    x_refs = (xf_ref, xb_ref)
    sm_refs = (smf_ref, smb_ref)
    y_refs = (yf_ref, yb_ref)
    sign = (1, -1)
    dif = _iota((q, q), 0) - _iota((q, q), 1)
    dif4 = _iota((q, 4 * q), 0) - (_iota((q, 4 * q), 1) & (q - 1))
    tri_bf = [jnp.where(dif * s >= 0, 1.0, 0.0).astype(BF16) for s in sign]
    tri4 = [dif4 * s >= 0 for s in sign]
    src = _iota((SMALL_W, D_GROUP), 0)
    expand = [jnp.where(src == 4 * d + (_iota((SMALL_W, D_GROUP), 1) >> 6), 1.0, 0.0).astype(BF16) for d in range(2)]
    src4 = _iota((SMALL_W, 4 * q), 0)
    expand4 = [jnp.where(src4 == 4 * d + (_iota((SMALL_W, 4 * q), 1) >> 7), 1.0, 0.0).astype(BF16) for d in range(2)]
    a_row = [-jnp.exp(alog_ref[d]) for d in range(2)]
    a_small = -jnp.exp(asm_ref[...])
    lane_c = _iota((q, q), 1) >> 6
    head_rows = (_iota((4 * q, D_GROUP), 0) >> 7) == (_iota((4 * q, D_GROUP), 1) >> 6)
    state_mask = (_iota((q, D_GROUP), 0) >> 6) == (_iota((q, D_GROUP), 1) >> 7)

    dt_exp, dta, dts = [], [], []
    for d in range(2):
        dt_all = _softplus(sm_refs[d][...] + dtb_ref[...])
        e = _mm_xe2(dt_all, expand[d])
        dt_exp.append(e)
        dta.append(e * a_row[d])
        dts.append(dt_all * a_small)

    for tasks in _groups([(d, c) for c in range(nc) for d in range(2)]):
        rows = [slice(c * q, (c + 1) * q) for _, c in tasks]
        cum = [_mm_ex(tri_bf[d], dta[d][r]) for (d, _), r in zip(tasks, rows)]
        cums = [_mm_ex(tri_bf[d], dts[d][r]) for (d, _), r in zip(tasks, rows)]
        col4 = [_mm_xe(cs, expand4[d]) for (d, _), cs in zip(tasks, cums)]
        cum_t = [cs.T for cs in cums]
        row4 = [jnp.concatenate([ct[4 * d + h:4 * d + h + 1, :] for h in range(N_HEADS)], axis=1)
                for (d, _), ct in zip(tasks, cum_t)]
        decay = [jnp.exp(jnp.where(tri4[d], c4 - r4, NEG_BIG)) for (d, _), c4, r4 in zip(tasks, col4, row4)]
        xc = [x_refs[d][r, :] for (d, _), r in zip(tasks, rows)]
        bm_bf = [_bf(x[:, 256:384]) for x in xc]
        gram = [[_dot_nt(_bf(jnp.where(lane_c == g, x[:, 384:512], 0.0)), b) for g in range(2)]
                for x, b in zip(xc, bm_bf)]
        xdt = [x[:, 0:256] * dt_exp[d][r] for (d, _), x, r in zip(tasks, xc, rows)]
        x_bd = [jnp.where(head_rows, jnp.concatenate([_bf(v)] * N_HEADS, axis=0), jnp.zeros((), BF16)) for v in xdt]
        scores = [jnp.concatenate([g[0], g[0], g[1], g[1]], axis=1) * dc for g, dc in zip(gram, decay)]
        y_intra = [_dot(_bf(sc), xb) for sc, xb in zip(scores, x_bd)]
        last = [cm[q - 1:q, :] if d == 0 else cm[0:1, :] for (d, _), cm in zip(tasks, cum)]
        upd = [_dot_tn(b, _bf(v * jnp.exp(la - cm))) for b, v, la, cm in zip(bm_bf, xdt, last, cum)]
        for t, (d, c) in enumerate(tasks):
            yi_s[d, rows[t], :] = y_intra[t]
            ec_s[d, rows[t], :] = jnp.exp(cum[t])
            upd_s[d, c] = jnp.where(state_mask, upd[t], 0.0)
            el_s[d, c] = jnp.exp(last[t])

    for j in range(nc):
        for d in range(2):
            c = j if d == 0 else nc - 1 - j
            r = slice(c * q, (c + 1) * q)
            s = s_ref[d]
            y_refs[d][r, :] = _dot(_bf(x_refs[d][r, 384:512]), _bf(s)) * ec_s[d, r, :] + yi_s[d, r, :]
            s_ref[d] = s * el_s[d, c] + upd_s[d, c]

    @pl.when(i == pl.num_programs(1) - 1)
    def _():
        hf_ref[...] = s_ref[...]


def _ssd_scan(lidx, xbc, small, dtb_row, alog_exp, alog_small, h0):
    bsz, seq, _ = xbc.shape
    tb = min(seq, 512)
    nb = seq // tb
    nc = tb // SSD_CHUNK
    fwd, bwd, par, state = _scan_specs(tb, nb)
    grid_spec = pltpu.PrefetchScalarGridSpec(
        num_scalar_prefetch=1,
        grid=(bsz, nb),
        in_specs=[fwd(512), fwd(SMALL_W), bwd(512), bwd(SMALL_W), par((1, SMALL_W)), par((2, 1, 256)),
                  par((1, SMALL_W)), state(128)],
        out_specs=[fwd(256), bwd(256), state(128)],
        scratch_shapes=[pltpu.VMEM((2, 128, 256), F32), pltpu.VMEM((2, tb, 256), F32), pltpu.VMEM((2, tb, 256), F32),
                        pltpu.VMEM((2, nc, 128, 256), F32), pltpu.VMEM((2, nc, 1, 256), F32)],
    )
    return pl.pallas_call(
        functools.partial(_ssd_kernel, nc=nc),
        out_shape=[jax.ShapeDtypeStruct((bsz, seq, 256), F32), jax.ShapeDtypeStruct((bsz, seq, 256), F32),
                   jax.ShapeDtypeStruct((bsz, 2, 128, 256), F32)],
        grid_spec=grid_spec,
        compiler_params=_cparams(),
        name="ssd_scan",
    )(lidx, xbc, small, xbc, small, dtb_row, alog_exp, alog_small, h0)


def _gdn_kernel(l_ref, qf_ref, kf_ref, vf_ref, smf_ref, qb_ref, kb_ref, vb_ref, smb_ref, dtb_ref, alog_ref, s0_ref,
                of_ref, ob_ref, sf_ref, s_ref, u_s, kt_s, wq_s, at_s, gl_s, *, nck):
    del l_ref
    cl = GDN_CHUNK
    i = pl.program_id(1)

    @pl.when(i == 0)
    def _():
        s_ref[...] = s0_ref[...]

    q_refs = (qf_ref, qb_ref)
    k_refs = (kf_ref, kb_ref)
    v_refs = (vf_ref, vb_ref)
    sm_refs = (smf_ref, smb_ref)
    o_refs = (of_ref, ob_ref)
    sign = (1, -1)
    ii = _iota((cl, D_GROUP), 0)
    jj = _iota((cl, D_GROUP), 1) & (cl - 1)
    low = [(ii - jj) * s >= 0 for s in sign]
    strict = [(ii - jj) * s > 0 for s in sign]
    diag = ii == jj
    blk16 = (ii >> 4) == (jj >> 4)
    blk32 = (ii >> 5) == (jj >> 5)
    off_masks = (blk32 & jnp.logical_not(blk16), jnp.logical_not(blk32))
    head_bd = (_iota((D_GROUP, D_GROUP), 0) >> 6) == (_iota((D_GROUP, D_GROUP), 1) >> 6)
    dif = _iota((cl, cl), 0) - _iota((cl, cl), 1)
    tri_bf = [jnp.where(dif * s >= 0, 1.0, 0.0).astype(BF16) for s in sign]
    src = _iota((SMALL_W, 2 * D_GROUP), 0)
    lane = _iota((SMALL_W, 2 * D_GROUP), 1)
    expand = [jnp.where(src == 8 + 8 * (lane >> 8) + 4 * d + ((lane >> 6) & 3), 1.0, 0.0).astype(BF16)
              for d in range(2)]
    a_row = [-jnp.exp(alog_ref[d]) for d in range(2)]

    def blockdiag(y_bf):
        return jnp.where(head_bd, jnp.concatenate([y_bf] * N_HEADS, axis=0), jnp.zeros((), BF16))

    def mul(xs, ys):
        return [_dot(_bf(x), blockdiag(_bf(y))) for x, y in zip(xs, ys)]

    g_exp, b_exp = [], []
    for d in range(2):
        sm = sm_refs[d][...]
        lane_s = _iota(sm.shape, 1)
        t = jnp.where((lane_s >= 8) & (lane_s < 16), _softplus(sm + dtb_ref[...]), _sigmoid(sm))
        e = _mm_xe2(t, expand[d])
        g_exp.append(e[:, 0:D_GROUP] * a_row[d])
        b_exp.append(e[:, D_GROUP:2 * D_GROUP])

    def chunk_local(tasks):
        ds = [d for d, _ in tasks]
        rows = [slice(c * cl, (c + 1) * cl) for _, c in tasks]
        gc = [_mm_ex(tri_bf[d], g_exp[d][r]) for d, r in zip(ds, rows)]
        gc_row = [jnp.sum(jnp.where(diag, g, 0.0), axis=0, keepdims=True) for g in gc]
        decay = [jnp.exp(jnp.where(low[d], g - gr, NEG_BIG)) for d, g, gr in zip(ds, gc, gc_row)]
        yield
        kc = [k_refs[d][r, :] for d, r in zip(ds, rows)]
        qs = [q_refs[d][r, :] * (HEAD_DIM ** -0.5) for d, r in zip(ds, rows)]
        be = [b_exp[d][r] for d, r in zip(ds, rows)]
        kb = [k * b for k, b in zip(kc, be)]
        k_bd = [blockdiag(_bf(k)) for k in kc]
        kq = [_dot_nt(_bf(jnp.concatenate([x, y], axis=0)), kd) for x, y, kd in zip(kb, qs, k_bd)]
        a = [jnp.where(strict[d], x[0:cl] * dc, 0.0) for d, x, dc in zip(ds, kq, decay)]
        attn = [x[cl:2 * cl] * dc for x, dc in zip(kq, decay)]
        yield
        ad = [jnp.where(blk16, x, 0.0) for x in a]
        a2 = mul(ad, ad)
        yield
        a4 = mul(a2, a2)
        yield
        a8 = mul(a4, a4)
        yield
        m = [-x for x in ad]
        for p in (a2, a4, a8):
            m = [mm + pp + mp for mm, pp, mp in zip(m, p, mul(m, p))]
            yield
        for off in off_masks:
            o = [jnp.where(off, x, 0.0) for x in a]
            x = [oo + mo for oo, mo in zip(o, mul(m, o))]
            yield
            m = [mm - (xx + xm) for mm, xx, xm in zip(m, x, mul(x, m))]
            yield
        rv = [v_refs[d][r, :] * b for d, r, b in zip(ds, rows, be)]
        rw = [x * jnp.exp(g) for x, g in zip(kb, gc)]
        u = [x + mx for x, mx in zip(rv, mul(m, rv))]
        yield
        w = [x + mx for x, mx in zip(rw, mul(m, rw))]
        for t, (d, c) in enumerate(tasks):
            g_last = gc[t][cl - 1:cl, :] if d == 0 else gc[t][0:1, :]
            u_s[d, rows[t], :] = u[t]
            kt_s[d, rows[t], :] = kc[t] * jnp.exp(g_last - gc[t])
            wq_s[d, c, 0:cl, :] = w[t]
            wq_s[d, c, cl:2 * cl, :] = qs[t] * jnp.exp(gc[t])
            at_s[d, rows[t], :] = attn[t]
            gl_s[d, c] = jnp.exp(g_last)
        yield

    def recurrence(steps):
        for j in steps:
            cs = [j, nck - 1 - j]
            rows = [slice(c * cl, (c + 1) * cl) for c in cs]
            s = [s_ref[d] for d in range(2)]
            ws = [_dot(_bf(wq_s[d, cs[d]]), _bf(s[d])) for d in range(2)]
            yield
            v_bf = [_bf(u_s[d, rows[d], :] - ws[d][0:cl]) for d in range(2)]
            upd = [_dot_tn(_bf(kt_s[d, rows[d], :]), v_bf[d]) for d in range(2)]
            o = [_dot(_bf(at_s[d, rows[d], :]), blockdiag(v_bf[d])) + ws[d][cl:2 * cl] for d in range(2)]
            yield
            for d in range(2):
                o_refs[d][rows[d], :] = o[d]
                s_ref[d] = s[d] * gl_s[d, cs[d]] + jnp.where(head_bd, upd[d], 0.0)
            yield

    def interleave(*gens):
        live = list(gens)
        while live:
            live = [g for g in live if next(g, StopIteration) is not StopIteration]

    lead = nck // 2 if 2 * nck > SCAN_GROUP else nck
    first = [(0, c) for c in range(lead)] + [(1, c) for c in range(nck - 1, nck - 1 - lead, -1)]
    second = [(0, c) for c in range(lead, nck)] + [(1, c) for c in range(nck - 1 - lead, -1, -1)]
    interleave(chunk_local(first))
    if second:
        interleave(chunk_local(second), recurrence(range(lead)))
        interleave(recurrence(range(lead, nck)))
    else:
        interleave(recurrence(range(nck)))

    @pl.when(i == pl.num_programs(1) - 1)
    def _():
        sf_ref[...] = s_ref[...]


def _gdn_scan(lidx, q, k, v, small, dtb_row, alog_exp, s0):
    bsz, seq, _ = q.shape
    tb = min(seq, 512)
    nb = seq // tb
    nck = tb // GDN_CHUNK
    fwd, bwd, par, state = _scan_specs(tb, nb)
    grid_spec = pltpu.PrefetchScalarGridSpec(
        num_scalar_prefetch=1,
        grid=(bsz, nb),
        in_specs=[fwd(256), fwd(256), fwd(256), fwd(SMALL_W), bwd(256), bwd(256), bwd(256), bwd(SMALL_W),
                  par((1, SMALL_W)), par((2, 1, 256)), state(256)],
        out_specs=[fwd(256), bwd(256), state(256)],
        scratch_shapes=[pltpu.VMEM((2, 256, 256), F32), pltpu.VMEM((2, tb, 256), F32), pltpu.VMEM((2, tb, 256), F32),
                        pltpu.VMEM((2, nck, 2 * GDN_CHUNK, 256), F32), pltpu.VMEM((2, tb, 256), F32),
                        pltpu.VMEM((2, nck, 1, 256), F32)],
    )
    return pl.pallas_call(
        functools.partial(_gdn_kernel, nck=nck),
        out_shape=[jax.ShapeDtypeStruct((bsz, seq, 256), F32), jax.ShapeDtypeStruct((bsz, seq, 256), F32),
                   jax.ShapeDtypeStruct((bsz, 2, 256, 256), F32)],
        grid_spec=grid_spec,
        compiler_params=_cparams(),
        name="gdn_scan",
    )(lidx, q, k, v, small, q, k, v, small, dtb_row, alog_exp, s0)


FFN_STRIP = 256
EDGE = 2 * HALO_F32


def _tail_kernel(l_ref, x_ref, mod_ref, yf_ref, yb_ref, xs_ref, z_ref, of_ref, ob_ref, gate_ref, yp_ref, ym_ref,
                 dsk_ref, snw_ref, gnw_ref, ones_ref, wo_ref, l1w_ref, l1b_ref, wup_ref, cw_ref, cb_ref, wd_ref, l2w_ref,
                 l2b_ref, x2_ref, win_s, b_s, x1_s, *, lagged):
    del l_ref
    dm = D_MODEL
    halo = HALO_F32
    t = x_ref.shape[0]
    i = pl.program_id(1)
    n = pl.num_programs(1) - 1
    n_strips = D_FF // FFN_STRIP
    pad = jnp.zeros((halo, FFN_STRIP), F32)

    def half_layer():
        mod = mod_ref[...]
        ys = (yf_ref[...] + yb_ref[...] + xs_ref[...] * dsk_ref[...]) * _silu(z_ref[...])
        ys = ys * lax.rsqrt(jnp.mean(ys * ys, axis=-1, keepdims=True) + RMS_EPS) * snw_ref[...]
        og = of_ref[...] + ob_ref[...]
        ms = _group_sum(og * og, ones_ref[...]) * (1.0 / HEAD_DIM)
        og = og * lax.rsqrt(ms + RMS_EPS) * gnw_ref[...] * _silu(gate_ref[...])
        proj = (_dot(_bf(yp_ref[...]), wo_ref[256:512, :]) + _dot(_bf(ym_ref[...]), wo_ref[768:1024, :])
                + _dot(_bf(ys), wo_ref[0:256, :]) + _dot(_bf(og), wo_ref[512:768, :]))
        x1 = _ln(DN_ALPHA * x_ref[...] + mod[:, 2 * dm:3 * dm] * proj) * l1w_ref[...] + l1b_ref[...]
        return x1, _bf(_ln(x1) * (1.0 + mod[:, 4 * dm:5 * dm]) + mod[:, 3 * dm:4 * dm])

    def up_strips(h2):
        up = lambda j: (_dot(h2, wup_ref[:, j * FFN_STRIP:(j + 1) * FFN_STRIP]),
                        _dot(h2, wup_ref[:, D_FF + j * FFN_STRIP:D_FF + (j + 1) * FFN_STRIP]))
        ab = up(0)
        for j in range(n_strips):
            ab_next = up(j + 1) if j + 1 < n_strips else None
            yield ab
            ab = ab_next

    def second_residual(x1, f):
        x2_ref[...] = _ln(DN_ALPHA * x1 + mod_ref[:, 5 * dm:6 * dm] * f) * l2w_ref[...] + l2b_ref[...]

    def ffn(produce, finish, first=False):
        if produce:
            x1, h2 = half_layer()
            strips = up_strips(h2)
        f = jnp.zeros((t, dm), F32)
        for j in range(n_strips):
            cols = slice(j * FFN_STRIP, (j + 1) * FFN_STRIP)
            if produce:
                a, b = next(strips)
            if finish:
                w = cw_ref[:, cols]
                head = a[0:halo] if produce else pad
                edge_win = jnp.concatenate([win_s[t - EDGE:t + halo, cols], head], axis=0)
                conv = jnp.concatenate([_conv_window(win_s[:, cols], w, FFN_CONV_W, halo, t - EDGE),
                                        _conv_window(edge_win, w, FFN_CONV_W, halo, EDGE)], axis=0)
                f = f + _dot(_bf(_silu(conv + cb_ref[:, cols]) * b_s[:, cols]), wd_ref[cols, :])
            if produce:
                win_s[0:halo, cols] = pad if first else win_s[t:t + halo, cols]
                win_s[halo:halo + t, cols] = a
                b_s[:, cols] = b
        if finish:
            second_residual(x1_s[...], f)
        if produce:
            x1_s[...] = x1

    if not lagged:
        x1, h2 = half_layer()
        f = jnp.zeros((t, dm), F32)
        for j, (a, b) in enumerate(up_strips(h2)):
            cols = slice(j * FFN_STRIP, (j + 1) * FFN_STRIP)
            conv = _conv_window(jnp.concatenate([pad, a, pad], axis=0), cw_ref[:, cols], FFN_CONV_W, halo, t)
            f = f + _dot(_bf(_silu(conv + cb_ref[:, cols]) * b), wd_ref[cols, :])
        second_residual(x1, f)
        return

    pl.when(i == 0)(lambda: ffn(True, False, first=True))
    pl.when((i > 0) & (i < n))(lambda: ffn(True, True))
    pl.when(i == n)(lambda: ffn(False, True))


def _tail(lidx, x, mods, y_f, y_b, xbc, z, o_f, o_b, gate, y_pool, y_gmlp, p, is_ctx):
    bsz, seq, _ = x.shape
    t = min(seq, ROW_TILE)
    n = seq // t
    cur, lag, steps = _tile_specs(t, n)
    grid_spec = pltpu.PrefetchScalarGridSpec(
        num_scalar_prefetch=1,
        grid=(bsz, steps),
        in_specs=[cur(D_MODEL), _mod_spec(is_ctx)] + [cur(256)] * 9
                 + [_layer_spec((1, 256)), _layer_spec((1, 256)), _layer_spec((1, 256)), _const_spec((256, 256)),
                    _layer_spec((D_MODEL, D_MODEL)), _layer_spec((1, D_MODEL)), _layer_spec((1, D_MODEL)),
                    _layer_spec((D_MODEL, 2 * D_FF)), _layer_spec((FFN_CONV_W, D_FF)), _layer_spec((1, D_FF)),
                    _layer_spec((D_FF, D_MODEL)), _layer_spec((1, D_MODEL)), _layer_spec((1, D_MODEL))],
        out_specs=lag(D_MODEL),
        scratch_shapes=[pltpu.VMEM((t + HALO_F32, D_FF), F32), pltpu.VMEM((t, D_FF), F32),
                        pltpu.VMEM((t, D_MODEL), F32)],
    )
    return pl.pallas_call(
        functools.partial(_tail_kernel, lagged=n > 1),
        out_shape=jax.ShapeDtypeStruct((bsz, seq, D_MODEL), F32),
        grid_spec=grid_spec,
        compiler_params=_cparams(),
        name="tail",
    )(lidx, x, mods, y_f, y_b, xbc, z, o_f, o_b, gate, y_pool, y_gmlp, p["d_exp"], p["ssd_nw"], p["gdn_nw"],
      _head_ones(), p["w_out"], p["ln1_w"], p["ln1_b"], p["ffn_up"], p["ffn_conv_w"], p["ffn_conv_b"], p["ffn_down"],
      p["ln2_w"], p["ln2_b"])


def _prep_params(w_in, ssd_conv_w, ssd_conv_b, ssd_dt_bias, ssd_a_log, ssd_d, ssd_norm_w, pool_w, pool_scale,
                 gdn_conv_w, gdn_dt_bias, gdn_a_log, gdn_norm_w, gmlp_ln_w, gmlp_ln_b, gmlp_ws, gmlp_bs, w_out,
                 ln1_w, ln1_b, ffn_up, ffn_conv_w, ffn_conv_b, ffn_down, ln2_w, ln2_b):
    nl = DEPTH
    cols = [w_in[:, :, 0:256], w_in[:, :, 256:768], w_in[:, :, 1032:1800], w_in[:, :, 776:1032],
            w_in[:, :, 1800:2056], w_in[:, :, 2072:2584], w_in[:, :, 768:776], w_in[:, :, 2056:2072],
            jnp.zeros((nl, D_MODEL, SMALL_W - 24), w_in.dtype)]
    row = lambda t: t.reshape(nl, 1, -1)
    pad_small = lambda t: jnp.concatenate([t, jnp.zeros((nl, SMALL_W - t.shape[-1]), F32)], axis=-1).reshape(nl, 1, SMALL_W)
    eye4 = jnp.eye(4, dtype=F32)
    pool_bd = (eye4[None, :, None, :, None] * pool_w[:, :, :, None, :]).reshape(nl, 256, 256)
    return dict(
        w_in=jnp.concatenate(cols, axis=-1).astype(BF16),
        conv_w=jnp.concatenate([ssd_conv_w, gdn_conv_w], axis=-1),
        conv_b=jnp.concatenate([ssd_conv_b, jnp.zeros((nl, 768), F32)], axis=-1).reshape(nl, 1, CONV_C),
        dtb_row=pad_small(jnp.concatenate([ssd_dt_bias.reshape(nl, 8), gdn_dt_bias.reshape(nl, 8)], axis=-1)),
        ssd_alog_exp=jnp.repeat(ssd_a_log, HEAD_DIM, axis=-1).reshape(nl, 2, 1, 256),
        ssd_alog_small=pad_small(ssd_a_log.reshape(nl, 8)),
        gdn_alog_exp=jnp.repeat(gdn_a_log, HEAD_DIM, axis=-1).reshape(nl, 2, 1, 256),
        d_exp=row(jnp.repeat(ssd_d, HEAD_DIM, axis=-1)),
        ssd_nw=row(ssd_norm_w),
        gdn_nw=row(jnp.tile(gdn_norm_w, (1, N_HEADS))),
        pool_wbd=pool_bd.astype(BF16),
        pool_scale=row(pool_scale),
        gmlp_ln_w=row(gmlp_ln_w),
        gmlp_ln_b=row(gmlp_ln_b),
        gmlp_ws=gmlp_ws.astype(BF16),
        gmlp_bs=jnp.repeat(jnp.swapaxes(gmlp_bs, 1, 2), HEAD_DIM, axis=-1),
        w_out=w_out.astype(BF16),
        ln1_w=row(ln1_w), ln1_b=row(ln1_b),
        ffn_up=ffn_up.astype(BF16),
        ffn_conv_w=ffn_conv_w,
        ffn_conv_b=row(ffn_conv_b),
        ffn_down=ffn_down.astype(BF16),
        ln2_w=row(ln2_w), ln2_b=row(ln2_b),
    )


def _stream_layer(lidx, x, mods, p, states0, is_ctx, full):
    z, gate, small, y_pool, y_gmlp, xbc, q, k, v = _front(lidx, x, mods, p, is_ctx)
    y_f, y_b, ssd_fin = _ssd_scan(lidx, xbc, small, p["dtb_row"], p["ssd_alog_exp"], p["ssd_alog_small"], states0[0])
    o_f, o_b, gdn_fin = _gdn_scan(lidx, q, k, v, small, p["dtb_row"], p["gdn_alog_exp"], states0[1])
    if not full:
        return None, (ssd_fin, gdn_fin)
    x2 = _tail(lidx, x, mods, y_f, y_b, xbc, z, o_f, o_b, gate, y_pool, y_gmlp, p, is_ctx)
    return x2, (ssd_fin, gdn_fin)


def kernel(x, c, ctx, c_ctx, w_mod, b_mod, w_in, ssd_conv_w, ssd_conv_b, ssd_dt_bias, ssd_a_log, ssd_d, ssd_norm_w, pool_w, pool_scale, gdn_conv_w, gdn_dt_bias, gdn_a_log, gdn_norm_w, gmlp_ln_w, gmlp_ln_b, gmlp_ws, gmlp_bs, w_out, ln1_w, ln1_b, ffn_up, ffn_conv_w, ffn_conv_b, ffn_down, ln2_w, ln2_b):
    bsz = x.shape[0]
    p = _prep_params(w_in, ssd_conv_w, ssd_conv_b, ssd_dt_bias, ssd_a_log, ssd_d, ssd_norm_w, pool_w, pool_scale,
                     gdn_conv_w, gdn_dt_bias, gdn_a_log, gdn_norm_w, gmlp_ln_w, gmlp_ln_b, gmlp_ws, gmlp_bs, w_out,
                     ln1_w, ln1_b, ffn_up, ffn_conv_w, ffn_conv_b, ffn_down, ln2_w, ln2_b)
    cs = jnp.concatenate([c_ctx[None, :], c, jnp.zeros((8 - 1 - bsz, D_MODEL), F32)], axis=0)
    mods = _modulation(cs, w_mod, b_mod).reshape(DEPTH, 8, 1, 6 * D_MODEL)
    zero_states = (jnp.zeros((bsz, 2, 128, 256), F32), jnp.zeros((bsz, 2, 256, 256), F32))
    lat, cx = x, ctx
    for l in range(DEPTH):
        lidx = jnp.full((1,), l, jnp.int32)
        cx_next, ctx_states = _stream_layer(lidx, cx, mods, p, zero_states, True, l < DEPTH - 1)
        lat, _ = _stream_layer(lidx, lat, mods, p, ctx_states, False, True)
        cx = cx_next
    return lat
```
